```python
import jax
import jax.numpy as jnp
from jax import lax
import numpy as np


D_MODEL = 1024
BATCH = 16
SEQ = 2048
DEPTH = 4

N_MIXERS = 2
N_MLA_LAYERS = (DEPTH + N_MIXERS - 1) // N_MIXERS
N_MLSTM_LAYERS = DEPTH // N_MIXERS

MLA_HEADS = D_MODEL // 128
MLA_NOPE_DIM = 128
MLA_ROPE_DIM = 64
MLA_V_DIM = 128
MLA_Q_RANK = 3 * D_MODEL // 8
MLA_KV_RANK = D_MODEL // 4
ROPE_BASE = 10000.0
Q_BLOCK = 128

MLSTM_HEADS = 4
MLSTM_V_DIM = D_MODEL // MLSTM_HEADS
MLSTM_QK_DIM = MLSTM_V_DIM // 2
MLSTM_CHUNK = 64

N_MEM = 256
MEM_HEADS = 4
MEM_HEAD_DIM = D_MODEL // MEM_HEADS

D_FF = -(-8 * D_MODEL // (3 * 256)) * 256

NORM_EPS = 1e-6

kernel_name = 'hybrid_mla_mlstm_memory_block'


def rmsnorm(x, g):
    xf = x.astype(jnp.float32)
    y = xf * lax.rsqrt(jnp.mean(xf * xf, axis=-1, keepdims=True) + NORM_EPS)
    return (y * g.astype(jnp.float32)).astype(x.dtype)


def rope_tables(positions):
    inv_freq = ROPE_BASE ** (-jnp.arange(0, MLA_ROPE_DIM, 2, dtype=jnp.float32) / MLA_ROPE_DIM)
    ang = positions.astype(jnp.float32)[..., None] * inv_freq
    return jnp.cos(ang), jnp.sin(ang)


def apply_rope(x, cos, sin):
    half = x.shape[-1] // 2
    x1, x2 = x[..., :half], x[..., half:]
    return jnp.concatenate([x1 * cos - x2 * sin, x2 * cos + x1 * sin], axis=-1).astype(x.dtype)


def mla_mixer(h, cos, sin, w_in, q_norm, w_uq, kv_norm, w_ukv, w_o):
    B, S, _ = h.shape
    H = MLA_HEADS
    proj = h @ w_in
    cq, ckv, k_rope = jnp.split(proj, [MLA_Q_RANK, MLA_Q_RANK + MLA_KV_RANK], axis=-1)
    q = (rmsnorm(cq, q_norm) @ w_uq).reshape(B, S, H, MLA_NOPE_DIM + MLA_ROPE_DIM)
    q_nope = q[..., :MLA_NOPE_DIM]
    q_rope = apply_rope(q[..., MLA_NOPE_DIM:], cos[:, :, None, :], sin[:, :, None, :])
    kv = (rmsnorm(ckv, kv_norm) @ w_ukv).reshape(B, S, H, MLA_NOPE_DIM + MLA_V_DIM)
    k_nope, v = kv[..., :MLA_NOPE_DIM], kv[..., MLA_NOPE_DIM:]
    k_rope = apply_rope(k_rope, cos, sin)
    scale = (MLA_NOPE_DIM + MLA_ROPE_DIM) ** -0.5
    nb = S // Q_BLOCK
    key_idx = jnp.arange(S)

    def to_blocks(t):
        return t.reshape(B, nb, Q_BLOCK, *t.shape[2:]).swapaxes(0, 1)

    def attend(args):
        qn, qr, q_idx = args
        s = (jnp.einsum('bqhd,bkhd->bhqk', qn, k_nope)
             + jnp.einsum('bqhr,bkr->bhqk', qr, k_rope)).astype(jnp.float32) * scale
        s = jnp.where(key_idx[None, :] <= q_idx[:, None], s, -jnp.inf)
        p = jax.nn.softmax(s, axis=-1).astype(v.dtype)
        return jnp.einsum('bhqk,bkhd->bqhd', p, v)

    o = lax.map(attend, (to_blocks(q_nope), to_blocks(q_rope), key_idx.reshape(nb, Q_BLOCK)))
    o = o.swapaxes(0, 1).reshape(B, S, H * MLA_V_DIM)
    return o @ w_o


def mlstm_mixer(h, w_in, b_gates, head_norm, w_o):
    B, S, _ = h.shape
    H, DK, DV, L = MLSTM_HEADS, MLSTM_QK_DIM, MLSTM_V_DIM, MLSTM_CHUNK
    f32 = jnp.float32
    proj = h @ w_in
    q, k, v, o_pre, gates = jnp.split(
        proj, [H * DK, 2 * H * DK, 2 * H * DK + H * DV, 2 * H * DK + 2 * H * DV], axis=-1)
    gates = gates.astype(f32) + b_gates.astype(f32)
    log_i = gates[..., :H]
    log_f = jax.nn.log_sigmoid(gates[..., H:])
    q = q.astype(f32).reshape(B, S, H, DK) * DK ** -0.5
    k = k.astype(f32).reshape(B, S, H, DK)
    v = v.astype(f32).reshape(B, S, H, DV)
    nc = S // L

    def chunk_vec(t):
        return t.reshape(B, nc, L, H, t.shape[-1]).transpose(1, 0, 3, 2, 4)

    def chunk_gate(t):
        return t.reshape(B, nc, L, H).transpose(1, 0, 3, 2)

    causal = jnp.tril(jnp.ones((L, L), dtype=bool))

    def step(carry, xs):
        C, n, m = carry
        qc, kc, vc, lic, lfc = xs
        b = jnp.cumsum(lfc, axis=-1)
        d_mat = jnp.where(causal, b[..., :, None] - b[..., None, :] + lic[..., None, :], -jnp.inf)
        g = b + m[..., None]
        mt = jnp.maximum(g, jnp.max(d_mat, axis=-1))
        w_intra = jnp.exp(d_mat - mt[..., None])
        w_inter = jnp.exp(g - mt)
        s = jnp.einsum('bhtd,bhsd->bhts', qc, kc) * w_intra
        num = (w_inter[..., None] * jnp.einsum('bhvd,bhtd->bhtv', C, qc)
               + jnp.einsum('bhts,bhsv->bhtv', s, vc))
        den = w_inter * jnp.einsum('bhd,bhtd->bht', n, qc) + jnp.sum(s, axis=-1)
        hc = num / jnp.maximum(jnp.abs(den), jnp.exp(-mt))[..., None]
        b_last = b[..., -1]
        m_new = mt[..., -1]
        w_k = jnp.exp(b_last[..., None] - b + lic - m_new[..., None])
        decay = jnp.exp(b_last + m - m_new)
        C = decay[..., None, None] * C + jnp.einsum('bhs,bhsv,bhsd->bhvd', w_k, vc, kc)
        n = decay[..., None] * n + jnp.einsum('bhs,bhsd->bhd', w_k, kc)
        return (C, n, m_new), hc

    init = (jnp.zeros((B, H, DV, DK), f32), jnp.zeros((B, H, DK), f32), jnp.zeros((B, H), f32))
    xs = (chunk_vec(q), chunk_vec(k), chunk_vec(v), chunk_gate(log_i), chunk_gate(log_f))
    _, hs = lax.scan(step, init, xs)
    hs = hs.transpose(1, 0, 3, 2, 4).reshape(B, S, H, DV)
    hs = rmsnorm(hs, head_norm)
    out = jax.nn.sigmoid(o_pre.astype(f32)).reshape(B, S, H, DV) * hs
    return out.reshape(B, S, H * DV).astype(h.dtype) @ w_o


def memory_xattn(h, mem_n, w_q, w_kv, w_o):
    B, S, _ = h.shape
    q = (h @ w_q).reshape(B, S, MEM_HEADS, MEM_HEAD_DIM)
    kv = (mem_n @ w_kv).reshape(B, mem_n.shape[1], 2, MEM_HEADS, MEM_HEAD_DIM)
    k, v = kv[:, :, 0], kv[:, :, 1]
    s = jnp.einsum('bqhd,bkhd->bhqk', q, k).astype(jnp.float32) * MEM_HEAD_DIM ** -0.5
    p = jax.nn.softmax(s, axis=-1).astype(v.dtype)
    o = jnp.einsum('bhqk,bkhd->bqhd', p, v).reshape(B, S, MEM_HEADS * MEM_HEAD_DIM)
    return o @ w_o


def swiglu(h, w_gate_up, w_down):
    gate, up = jnp.split(h @ w_gate_up, 2, axis=-1)
    return (jax.nn.silu(gate) * up) @ w_down


def _dense(key, shape, fan_in):
    return jax.random.normal(key, shape, jnp.float32) * fan_in ** -0.5


def _gain(key, shape):
    return 1.0 + 0.02 * jax.random.normal(key, shape, jnp.float32)


def setup_inputs(seed: int = 0) -> dict:
    key = jax.random.key(seed)
    ks = jax.random.split(key, 32)
    D = D_MODEL
    mla_in_cols = MLA_Q_RANK + MLA_KV_RANK + MLA_ROPE_DIM
    mlstm_in_cols = 2 * MLSTM_HEADS * MLSTM_QK_DIM + 2 * MLSTM_HEADS * MLSTM_V_DIM + 2 * MLSTM_HEADS
    x = jax.random.normal(ks[0], (BATCH, SEQ, D), jnp.float32)
    mem = jax.random.normal(ks[1], (BATCH, N_MEM, D), jnp.float32)
    offsets = jax.random.randint(ks[2], (BATCH, 1), 0, 4096, dtype=jnp.int32)
    positions = offsets + jnp.arange(SEQ, dtype=jnp.int32)[None, :]
    b_input = 0.1 * jax.random.normal(ks[3], (N_MLSTM_LAYERS, MLSTM_HEADS), jnp.float32)
    b_forget = (jnp.linspace(3.0, 6.0, MLSTM_HEADS, dtype=jnp.float32)[None, :]
                + 0.1 * jax.random.normal(ks[4], (N_MLSTM_LAYERS, MLSTM_HEADS), jnp.float32))
    return {
        'x': x,
        'mem': mem,
        'positions': positions,
        'mla_w_in': _dense(ks[5], (N_MLA_LAYERS, D, mla_in_cols), D),
        'mla_q_norm': _gain(ks[6], (N_MLA_LAYERS, MLA_Q_RANK)),
        'mla_w_uq': _dense(ks[7], (N_MLA_LAYERS, MLA_Q_RANK, MLA_HEADS * (MLA_NOPE_DIM + MLA_ROPE_DIM)), MLA_Q_RANK),
        'mla_kv_norm': _gain(ks[8], (N_MLA_LAYERS, MLA_KV_RANK)),
        'mla_w_ukv': _dense(ks[9], (N_MLA_LAYERS, MLA_KV_RANK, MLA_HEADS * (MLA_NOPE_DIM + MLA_V_DIM)), MLA_KV_RANK),
        'mla_w_o': _dense(ks[10], (N_MLA_LAYERS, MLA_HEADS * MLA_V_DIM, D), MLA_HEADS * MLA_V_DIM),
        'mlstm_w_in': _dense(ks[11], (N_MLSTM_LAYERS, D, mlstm_in_cols), D),
        'mlstm_b_gates': jnp.concatenate([b_input, b_forget], axis=-1),
        'mlstm_head_norm': _gain(ks[12], (N_MLSTM_LAYERS, MLSTM_HEADS, MLSTM_V_DIM)),
        'mlstm_w_o': _dense(ks[13], (N_MLSTM_LAYERS, MLSTM_HEADS * MLSTM_V_DIM, D), MLSTM_HEADS * MLSTM_V_DIM),
        'norm_mix_pre': _gain(ks[14], (DEPTH, D)),
        'norm_mix_post': _gain(ks[15], (DEPTH, D)),
        'norm_mem_q': _gain(ks[16], (DEPTH, D)),
        'norm_mem_kv': _gain(ks[17], (DEPTH, D)),
        'norm_mem_post': _gain(ks[18], (DEPTH, D)),
        'norm_ffn_pre': _gain(ks[19], (DEPTH, D)),
        'norm_ffn_post': _gain(ks[20], (DEPTH, D)),
        'mem_w_q': _dense(ks[21], (DEPTH, D, MEM_HEADS * MEM_HEAD_DIM), D),
        'mem_w_kv': _dense(ks[22], (DEPTH, D, 2 * MEM_HEADS * MEM_HEAD_DIM), D),
        'mem_w_o': _dense(ks[23], (DEPTH, MEM_HEADS * MEM_HEAD_DIM, D), MEM_HEADS * MEM_HEAD_DIM),
        'ffn_w_gate_up': _dense(ks[24], (DEPTH, D, 2 * D_FF), D),
        'ffn_w_down': _dense(ks[25], (DEPTH, D_FF, D), D_FF),
    }


def reference(x, mem, positions, mla_w_in, mla_q_norm, mla_w_uq, mla_kv_norm, mla_w_ukv, mla_w_o,
              mlstm_w_in, mlstm_b_gates, mlstm_head_norm, mlstm_w_o,
              norm_mix_pre, norm_mix_post, norm_mem_q, norm_mem_kv, norm_mem_post,
              norm_ffn_pre, norm_ffn_post, mem_w_q, mem_w_kv, mem_w_o, ffn_w_gate_up, ffn_w_down):
    cos, sin = rope_tables(positions)
    for i in range(DEPTH):
        j = i // N_MIXERS
        h = rmsnorm(x, norm_mix_pre[i])
        if i % N_MIXERS == 0:
            h = mla_mixer(h, cos, sin, mla_w_in[j], mla_q_norm[j], mla_w_uq[j],
                          mla_kv_norm[j], mla_w_ukv[j], mla_w_o[j])
        else:
            h = mlstm_mixer(h, mlstm_w_in[j], mlstm_b_gates[j], mlstm_head_norm[j], mlstm_w_o[j])
        x = x + rmsnorm(h, norm_mix_post[i])
        h = memory_xattn(rmsnorm(x, norm_mem_q[i]), rmsnorm(mem, norm_mem_kv[i]),
                         mem_w_q[i], mem_w_kv[i], mem_w_o[i])
        x = x + rmsnorm(h, norm_mem_post[i])
        h = swiglu(rmsnorm(x, norm_ffn_pre[i]), ffn_w_gate_up[i], ffn_w_down[i])
        x = x + rmsnorm(h, norm_ffn_post[i])
    return x
```

```python
import functools

import jax
import jax.numpy as jnp
from jax import lax
from jax.experimental import pallas as pl
from jax.experimental.pallas import tpu as pltpu

F32 = jnp.float32
BF16 = jnp.bfloat16

D_MODEL = 1024
DEPTH = 4
NORM_EPS = 1e-6

MLA_HEADS = 8
MLA_NOPE = 128
MLA_ROPE = 64
MLA_V = 128
MLA_Q_RANK = 384
MLA_KV_RANK = 256
MLA_QK_PAD = 256
ROPE_BASE = 10000.0

MLSTM_HEADS = 4
MLSTM_DV = 256
MLSTM_DK = 128
MLSTM_CHUNK = 256
MLSTM_STATE_W = MLSTM_DV + 128

N_MEM = 256
MEM_HEADS = 4
MEM_DH = 256

D_FF = 2816
FF_CHUNK = 256

ROW_TILE = 512
ATTN_TILE = 512
LANE = 128
VMEM_LIMIT = 56 * 1024 * 1024


def _rms(x, g):
    ms = jnp.mean(x * x, axis=-1, keepdims=True)
    return x * lax.rsqrt(ms + NORM_EPS) * g


def _dot(a, b):
    return jnp.dot(a, b, preferred_element_type=F32)


def _dot_nt(a, b):
    return lax.dot_general(a, b, (((1,), (1,)), ((), ())), preferred_element_type=F32)


def _const_spec(shape):
    nd = len(shape)
    return pl.BlockSpec(shape, lambda *_: (0,) * nd, pipeline_mode=pl.Buffered(1))


def _params(*sem):
    return pltpu.CompilerParams(dimension_semantics=sem, vmem_limit_bytes=VMEM_LIMIT)


def _rope_table_kernel(pos_ref, freq_ref, cmask_ref, smask_ref, cos_ref, sin_ref):
    ang = pos_ref[...].astype(F32) * freq_ref[...]
    cos_ref[...] = jnp.cos(ang) * cmask_ref[...]
    sin_ref[...] = jnp.sin(ang) * smask_ref[...]


def _rope_tables(pos_col):
    t = pos_col.shape[0]
    half = MLA_ROPE // 2
    inv_freq = ROPE_BASE ** (-jnp.arange(0, MLA_ROPE, 2, dtype=F32) / MLA_ROPE)
    zeros = jnp.zeros((LANE - MLA_ROPE,), F32)
    freq = jnp.concatenate([inv_freq, inv_freq, zeros])[None, :]
    cmask = jnp.concatenate([jnp.ones((MLA_ROPE,), F32), zeros])[None, :]
    smask = jnp.concatenate([-jnp.ones((half,), F32), jnp.ones((half,), F32), zeros])[None, :]
    tm = 1024
    row = pl.BlockSpec((1, LANE), lambda i: (0, 0))
    return pl.pallas_call(
        _rope_table_kernel,
        grid=(t // tm,),
        in_specs=[pl.BlockSpec((tm, 1), lambda i: (i, 0)), row, row, row],
        out_specs=[pl.BlockSpec((tm, LANE), lambda i: (i, 0))] * 2,
        out_shape=[jax.ShapeDtypeStruct((t, LANE), F32)] * 2,
        compiler_params=_params("parallel"),
        name="rope_tables",
    )(pos_col, freq, cmask, smask)


def _norm_matmul_kernel(x_ref, g_ref, w_ref, o_ref):
    hn = _rms(x_ref[...], g_ref[...]).astype(BF16)
    o_ref[...] = _dot(hn, w_ref[...]).astype(o_ref.dtype)


def _norm_matmul(x, g, w, tm=ROW_TILE):
    t, d = x.shape
    n = w.shape[1]
    return pl.pallas_call(
        _norm_matmul_kernel,
        grid=(t // tm,),
        in_specs=[pl.BlockSpec((tm, d), lambda i: (i, 0)), _const_spec((1, d)), _const_spec((d, n))],
        out_specs=pl.BlockSpec((tm, n), lambda i: (i, 0)),
        out_shape=jax.ShapeDtypeStruct((t, n), BF16),
        compiler_params=_params("parallel"),
        name="norm_matmul",
    )(x, g, w)


def _proj_norm_res_kernel(a_ref, w_ref, g_ref, x_ref, o_ref):
    h = _dot(a_ref[...], w_ref[...])
    o_ref[...] = x_ref[...] + _rms(h, g_ref[...])


def _proj_norm_res(a, w, g, x, tm=ROW_TILE):
    t, d = x.shape
    k = a.shape[1]
    return pl.pallas_call(
        _proj_norm_res_kernel,
        grid=(t // tm,),
        in_specs=[pl.BlockSpec((tm, k), lambda i: (i, 0)), _const_spec((k, d)), _const_spec((1, d)),
                  pl.BlockSpec((tm, d), lambda i: (i, 0))],
        out_specs=pl.BlockSpec((tm, d), lambda i: (i, 0)),
        out_shape=jax.ShapeDtypeStruct((t, d), F32),
        compiler_params=_params("parallel"),
        name="proj_norm_res",
    )(a, w, g, x)


def _mla_proj_kernel(x_ref, cos_ref, sin_ref, gpre_ref, win_ref, gq_ref, wqn_ref, wqr_ref, wqs_ref,
                     gkv_ref, wkv_ref, q_ref, k_ref, v_ref):
    hn = _rms(x_ref[...], gpre_ref[...]).astype(BF16)
    proj = _dot(hn, win_ref[...])
    cq = proj[:, :MLA_Q_RANK]
    ckv = proj[:, MLA_Q_RANK:MLA_Q_RANK + MLA_KV_RANK]
    kr = proj[:, MLA_Q_RANK + MLA_KV_RANK:MLA_Q_RANK + MLA_KV_RANK + LANE]
    krs = proj[:, MLA_Q_RANK + MLA_KV_RANK + LANE:]
    cos_t = cos_ref[...]
    sin_t = sin_ref[...]
    k_rope = (kr * cos_t + krs * sin_t).astype(BF16)

    cqn = _rms(cq, gq_ref[...]).astype(BF16)
    q_nope = _dot(cqn, wqn_ref[...])
    q_r = _dot(cqn, wqr_ref[...])
    q_s = _dot(cqn, wqs_ref[...])
    ckvn = _rms(ckv, gkv_ref[...]).astype(BF16)
    kv = _dot(ckvn, wkv_ref[...])

    for h in range(MLA_HEADS):
        lo, hi = h * LANE, (h + 1) * LANE
        base = h * MLA_QK_PAD
        q_ref[:, base:base + LANE] = q_nope[:, lo:hi].astype(BF16)
        q_ref[:, base + LANE:base + 2 * LANE] = (q_r[:, lo:hi] * cos_t + q_s[:, lo:hi] * sin_t).astype(BF16)
        k_ref[:, base:base + LANE] = kv[:, lo:hi].astype(BF16)
        k_ref[:, base + LANE:base + 2 * LANE] = k_rope
    v_ref[...] = kv[:, MLA_HEADS * MLA_NOPE:].astype(BF16)


def _mla_proj(x, cos_t, sin_t, gpre, win, gq, wqn, wqr, wqs, gkv, wkv, tm=ROW_TILE):
    t, d = x.shape
    hq = MLA_HEADS * MLA_QK_PAD
    hv = MLA_HEADS * MLA_V
    rows = lambda n: pl.BlockSpec((tm, n), lambda i: (i, 0))
    return pl.pallas_call(
        _mla_proj_kernel,
        grid=(t // tm,),
        in_specs=[rows(d), rows(LANE), rows(LANE), _const_spec(gpre.shape), _const_spec(win.shape),
                  _const_spec(gq.shape), _const_spec(wqn.shape), _const_spec(wqr.shape),
                  _const_spec(wqs.shape), _const_spec(gkv.shape), _const_spec(wkv.shape)],
        out_specs=[rows(hq), rows(hq), rows(hv)],
        out_shape=[jax.ShapeDtypeStruct((t, hq), BF16), jax.ShapeDtypeStruct((t, hq), BF16),
                   jax.ShapeDtypeStruct((t, hv), BF16)],
        compiler_params=_params("parallel"),
        name="mla_proj",
    )(x, cos_t, sin_t, gpre, win, gq, wqn, wqr, wqs, gkv, wkv)


def _mla_attn_kernel(q_ref, k_ref, v_ref, o_ref, *, tile, scale):
    i = pl.program_id(2)
    q = q_ref[...]

    def step(off, carry, mask):
        m, l, acc = carry
        kb = k_ref[pl.ds(off, tile), :]
        vb = v_ref[pl.ds(off, tile), :]
        s = _dot_nt(q, kb) * scale
        if mask is not None:
            s = jnp.where(mask, s, -jnp.inf)
        m_new = jnp.maximum(m, jnp.max(s, axis=-1, keepdims=True))
        alpha = jnp.exp(m - m_new)
        p = jnp.exp(s - m_new)
        l = alpha * l + jnp.sum(p, axis=-1, keepdims=True)
        acc = alpha * acc + _dot(p.astype(BF16), vb)
        return m_new, l, acc

    init = (jnp.full((tile, 1), -jnp.inf, F32), jnp.zeros((tile, 1), F32), jnp.zeros((tile, MLA_V), F32))
    carry = lax.fori_loop(0, i, lambda j, c: step(pl.multiple_of(j * tile, tile), c, None), init)
    row = lax.broadcasted_iota(jnp.int32, (tile, tile), 0)
    col = lax.broadcasted_iota(jnp.int32, (tile, tile), 1)
    _, l, acc = step(pl.multiple_of(i * tile, tile), carry, col <= row)
    o_ref[...] = (acc / l).astype(o_ref.dtype)


def _mla_attn(q, k, v, batch, seq, tile=ATTN_TILE):
    t = q.shape[0]
    nq = seq // tile
    scale = (MLA_NOPE + MLA_ROPE) ** -0.5
    return pl.pallas_call(
        functools.partial(_mla_attn_kernel, tile=tile, scale=scale),
        grid=(batch, MLA_HEADS, nq),
        in_specs=[pl.BlockSpec((tile, MLA_QK_PAD), lambda b, h, i: (b * nq + i, h)),
                  pl.BlockSpec((seq, MLA_QK_PAD), lambda b, h, i: (b, h)),
                  pl.BlockSpec((seq, MLA_V), lambda b, h, i: (b, h))],
        out_specs=pl.BlockSpec((tile, MLA_V), lambda b, h, i: (b * nq + i, h)),
        out_shape=jax.ShapeDtypeStruct((t, MLA_HEADS * MLA_V), BF16),
        compiler_params=_params("parallel", "parallel", "arbitrary"),
        name="mla_attn",
    )(q, k, v)


def _mlstm_proj_kernel(x_ref, gpre_ref, w_ref, wkt_ref, wg_ref, wgt_ref, bg_ref, bgt_ref,
                       q_ref, kt_ref, v_ref, op_ref, gc_ref, gr_ref):
    hn = _rms(x_ref[...], gpre_ref[...]).astype(BF16)
    nq = MLSTM_HEADS * MLSTM_DK
    nv = MLSTM_HEADS * MLSTM_DV
    proj = _dot(hn, w_ref[...])
    q_ref[...] = (proj[:, :nq] * MLSTM_DK ** -0.5).astype(BF16)
    v_ref[...] = proj[:, nq:nq + nv].astype(BF16)
    op_ref[...] = proj[:, nq + nv:]
    kt_ref[...] = _dot_nt(wkt_ref[...], hn).astype(BF16)
    gc_ref[...] = _dot(hn, wg_ref[...]) + bg_ref[...]
    gr_ref[...] = _dot_nt(wgt_ref[...], hn) + bgt_ref[...]


def _mlstm_proj(x, gpre, w, wkt, wg, wgt, bg, bgt, tm=ROW_TILE):
    t, d = x.shape
    nq = MLSTM_HEADS * MLSTM_DK
    nv = MLSTM_HEADS * MLSTM_DV
    ng = 2 * MLSTM_HEADS
    rows = lambda n: pl.BlockSpec((tm, n), lambda i: (i, 0))
    cols = lambda n: pl.BlockSpec((n, tm), lambda i: (0, i))
    return pl.pallas_call(
        _mlstm_proj_kernel,
        grid=(t // tm,),
        in_specs=[rows(d), _const_spec(gpre.shape), _const_spec(w.shape), _const_spec(wkt.shape),
                  _const_spec(wg.shape), _const_spec(wgt.shape), _const_spec(bg.shape),
                  _const_spec(bgt.shape)],
        out_specs=[rows(nq), cols(nq), rows(nv), rows(nv), rows(LANE), cols(ng)],
        out_shape=[jax.ShapeDtypeStruct((t, nq), BF16), jax.ShapeDtypeStruct((nq, t), BF16),
                   jax.ShapeDtypeStruct((t, nv), BF16), jax.ShapeDtypeStruct((t, nv), F32),
                   jax.ShapeDtypeStruct((t, LANE), F32), jax.ShapeDtypeStruct((ng, t), F32)],
        compiler_params=_params("parallel"),
        name="mlstm_proj",
    )(x, gpre, w, wkt, wg, wgt, bg, bgt)


def _log_sigmoid(x):
    return jnp.minimum(x, 0.0) - jnp.log1p(jnp.exp(-jnp.abs(x)))


def _mlstm_scan_kernel(q_ref, kt_ref, v_ref, op_ref, gc_ref, gr_ref, hn_ref, o_ref, state_ref, m_ref):
    L = MLSTM_CHUNK
    H, DK, DV = MLSTM_HEADS, MLSTM_DK, MLSTM_DV

    @pl.when(pl.program_id(1) == 0)
    def _():
        state_ref[...] = jnp.zeros_like(state_ref)
        m_ref[...] = jnp.zeros_like(m_ref)

    gc = gc_ref[...]
    gr = gr_ref[...]
    row = lax.broadcasted_iota(jnp.int32, (L, L), 0)
    col = lax.broadcasted_iota(jnp.int32, (L, L), 1)
    causal = row >= col
    tril = causal.astype(F32)
    triu = (row <= col).astype(F32)
    hi = lax.Precision.HIGHEST
    b_cols = jnp.dot(tril, _log_sigmoid(gc), precision=hi, preferred_element_type=F32)
    b_rows = jnp.dot(_log_sigmoid(gr), triu, precision=hi, preferred_element_type=F32)

    for h in range(H):
        b_c = b_cols[:, H + h:H + h + 1]
        li_c = gc[:, h:h + 1]
        b_r = b_rows[H + h:H + h + 1, :]
        li_r = gr[h:h + 1, :]
        m = m_ref[h][:, 0:1]
        d = jnp.where(causal, b_c - b_r + li_r, -jnp.inf)
        g = b_c + m
        mt = jnp.maximum(g, jnp.max(d, axis=-1, keepdims=True))
        w_intra = jnp.exp(d - mt)
        w_inter = jnp.exp(g - mt)

        q = q_ref[:, h * DK:(h + 1) * DK]
        kt = kt_ref[h * DK:(h + 1) * DK, :]
        v = v_ref[:, h * DV:(h + 1) * DV]
        s = _dot(q, kt) * w_intra
        inter = _dot(q, state_ref[h].astype(BF16))
        num = w_inter * inter[:, :DV] + _dot(s.astype(BF16), v)
        den = w_inter * inter[:, DV:DV + 1] + jnp.sum(s, axis=-1, keepdims=True)
        hc = num / jnp.maximum(jnp.abs(den), jnp.exp(-mt))

        out = jax.nn.sigmoid(op_ref[:, h * DV:(h + 1) * DV]) * _rms(hc, hn_ref[h])
        o_ref[:, h * DV:(h + 1) * DV] = out.astype(o_ref.dtype)

        b_last = b_c[L - 1:L, :]
        m_new = mt[L - 1:L, :]
        w_k = jnp.exp(b_last - b_c + li_c - m_new)
        decay = jnp.exp(b_last + m - m_new)
        v_ext = jnp.concatenate([(v.astype(F32) * w_k).astype(BF16),
                                 jnp.broadcast_to(w_k, (L, LANE)).astype(BF16)], axis=1)
        state_ref[h] = decay * state_ref[h] + _dot(kt, v_ext)
        m_ref[h] = jnp.broadcast_to(m_new, (1, LANE))


def _mlstm_scan(q, kt, v, op, gc, gr, hnorm, batch, seq):
    t = q.shape[0]
    L = MLSTM_CHUNK
    nc = seq // L
    nq = MLSTM_HEADS * MLSTM_DK
    nv = MLSTM_HEADS * MLSTM_DV
    rows = lambda n: pl.BlockSpec((L, n), lambda b, c: (b * nc + c, 0))
    cols = lambda n: pl.BlockSpec((n, L), lambda b, c: (0, b * nc + c))
    return pl.pallas_call(
        _mlstm_scan_kernel,
        grid=(batch, nc),
        in_specs=[rows(nq), cols(nq), rows(nv), rows(nv), rows(LANE), cols(2 * MLSTM_HEADS),
                  _const_spec(hnorm.shape)],
        out_specs=rows(nv),
        out_shape=jax.ShapeDtypeStruct((t, nv), BF16),
        scratch_shapes=[pltpu.VMEM((MLSTM_HEADS, MLSTM_DK, MLSTM_STATE_W), F32),
                        pltpu.VMEM((MLSTM_HEADS, 1, LANE), F32)],
        compiler_params=_params("parallel", "arbitrary"),
        name="mlstm_scan",
    )(q, kt, v, op, gc, gr, hnorm)


def _mem_xattn_kernel(x_ref, gq_ref, wq_ref, kv_ref, wo_ref, gpost_ref, o_ref, oc_ref):
    x = x_ref[...]
    hn = _rms(x, gq_ref[...]).astype(BF16)
    q = (_dot(hn, wq_ref[...]) * MEM_DH ** -0.5).astype(BF16)
    nk = MEM_HEADS * MEM_DH
    for h in range(MEM_HEADS):
        lo, hi = h * MEM_DH, (h + 1) * MEM_DH
        s = _dot_nt(q[:, lo:hi], kv_ref[:, lo:hi])
        e = jnp.exp(s - jnp.max(s, axis=-1, keepdims=True))
        p = e / jnp.sum(e, axis=-1, keepdims=True)
        oc_ref[:, lo:hi] = _dot(p.astype(BF16), kv_ref[:, nk + lo:nk + hi]).astype(BF16)
    o_ref[...] = x + _rms(_dot(oc_ref[...], wo_ref[...]), gpost_ref[...])


def _mem_xattn(x, gq, wq, kv, wo, gpost, batch, seq, tm=ROW_TILE):
    t, d = x.shape
    nt = seq // tm
    return pl.pallas_call(
        _mem_xattn_kernel,
        grid=(batch, nt),
        in_specs=[pl.BlockSpec((tm, d), lambda b, i: (b * nt + i, 0)), _const_spec(gq.shape),
                  _const_spec(wq.shape), pl.BlockSpec((N_MEM, kv.shape[1]), lambda b, i: (b, 0)),
                  _const_spec(wo.shape), _const_spec(gpost.shape)],
        out_specs=pl.BlockSpec((tm, d), lambda b, i: (b * nt + i, 0)),
        out_shape=jax.ShapeDtypeStruct((t, d), F32),
        scratch_shapes=[pltpu.VMEM((tm, MEM_HEADS * MEM_DH), BF16)],
        compiler_params=_params("parallel", "parallel"),
        name="mem_xattn",
    )(x, gq, wq, kv, wo, gpost)


def _swiglu_kernel(x_ref, gpre_ref, wg_ref, wu_ref, wd_ref, gpost_ref, o_ref, act_ref):
    x = x_ref[...]
    hn = _rms(x, gpre_ref[...]).astype(BF16)
    for c in range(D_FF // FF_CHUNK):
        lo, hi = c * FF_CHUNK, (c + 1) * FF_CHUNK
        gate = _dot(hn, wg_ref[:, lo:hi])
        up = _dot(hn, wu_ref[:, lo:hi])
        act_ref[:, lo:hi] = (gate * jax.nn.sigmoid(gate) * up).astype(BF16)
    o_ref[...] = x + _rms(_dot(act_ref[...], wd_ref[...]), gpost_ref[...])


def _swiglu(x, gpre, wg, wu, wd, gpost, tm=ROW_TILE):
    t, d = x.shape
    return pl.pallas_call(
        _swiglu_kernel,
        grid=(t // tm,),
        in_specs=[pl.BlockSpec((tm, d), lambda i: (i, 0)), _const_spec(gpre.shape), _const_spec(wg.shape),
                  _const_spec(wu.shape), _const_spec(wd.shape), _const_spec(gpost.shape)],
        out_specs=pl.BlockSpec((tm, d), lambda i: (i, 0)),
        out_shape=jax.ShapeDtypeStruct((t, d), F32),
        scratch_shapes=[pltpu.VMEM((tm, D_FF), BF16)],
        compiler_params=_params("parallel"),
        name="swiglu",
    )(x, gpre, wg, wu, wd, gpost)


def _pad_cols(w, n):
    return jnp.pad(w, ((0, 0), (0, n - w.shape[1])))


def _swap_halves(w):
    half = w.shape[-1] // 2
    return jnp.concatenate([w[..., half:], w[..., :half]], axis=-1)


def _mla_weights(w_in, w_uq, w_ukv):
    r = MLA_Q_RANK + MLA_KV_RANK
    w_kr = w_in[:, r:]
    win = jnp.concatenate([w_in[:, :r], _pad_cols(w_kr, LANE), _pad_cols(_swap_halves(w_kr), LANE)], axis=1)
    uq = w_uq.reshape(MLA_Q_RANK, MLA_HEADS, MLA_NOPE + MLA_ROPE)
    wqn = uq[:, :, :MLA_NOPE].reshape(MLA_Q_RANK, MLA_HEADS * LANE)
    rope = uq[:, :, MLA_NOPE:]
    pad = ((0, 0), (0, 0), (0, LANE - MLA_ROPE))
    wqr = jnp.pad(rope, pad).reshape(MLA_Q_RANK, MLA_HEADS * LANE)
    wqs = jnp.pad(_swap_halves(rope), pad).reshape(MLA_Q_RANK, MLA_HEADS * LANE)
    ukv = w_ukv.reshape(MLA_KV_RANK, MLA_HEADS, 2, MLA_NOPE)
    wkv = ukv.transpose(0, 2, 1, 3).reshape(MLA_KV_RANK, 2 * MLA_HEADS * MLA_NOPE)
    return tuple(w.astype(BF16) for w in (win, wqn, wqr, wqs, wkv))


def _mlstm_weights(w_in, b_gates):
    nq = MLSTM_HEADS * MLSTM_DK
    nv = MLSTM_HEADS * MLSTM_DV
    w_q, w_k = w_in[:, :nq], w_in[:, nq:2 * nq]
    w_vo = w_in[:, 2 * nq:2 * nq + 2 * nv]
    w_g = w_in[:, 2 * nq + 2 * nv:]
    w_main = jnp.concatenate([w_q, w_vo], axis=1).astype(BF16)
    bg = _pad_cols(b_gates[None, :], LANE)
    return (w_main, w_k.T.astype(BF16), _pad_cols(w_g, LANE).astype(BF16), w_g.T.astype(BF16),
            bg, b_gates[:, None])


def kernel(x, mem, positions, mla_w_in, mla_q_norm, mla_w_uq, mla_kv_norm, mla_w_ukv, mla_w_o, mlstm_w_in, mlstm_b_gates, mlstm_head_norm, mlstm_w_o, norm_mix_pre, norm_mix_post, norm_mem_q, norm_mem_kv, norm_mem_post, norm_ffn_pre, norm_ffn_post, mem_w_q, mem_w_kv, mem_w_o, ffn_w_gate_up, ffn_w_down):
    batch, seq, d = x.shape
    t = batch * seq
    xf = x.reshape(t, d)
    memf = mem.reshape(batch * mem.shape[1], d)
    cos_t, sin_t = _rope_tables(positions.reshape(t, 1))
    gain = lambda g: g[None, :]

    for i in range(DEPTH):
        j = i // 2
        if i % 2 == 0:
            win, wqn, wqr, wqs, wkv = _mla_weights(mla_w_in[j], mla_w_uq[j], mla_w_ukv[j])
            q, k, v = _mla_proj(xf, cos_t, sin_t, gain(norm_mix_pre[i]), win, gain(mla_q_norm[j]),
                                wqn, wqr, wqs, gain(mla_kv_norm[j]), wkv)
            a = _mla_attn(q, k, v, batch, seq)
            w_o = mla_w_o[j]
        else:
            w_main, wkt, wg, wgt, bg, bgt = _mlstm_weights(mlstm_w_in[j], mlstm_b_gates[j])
            q, kt, v, op, gc, gr = _mlstm_proj(xf, gain(norm_mix_pre[i]), w_main, wkt, wg, wgt, bg, bgt)
            a = _mlstm_scan(q, kt, v, op, gc, gr, mlstm_head_norm[j][:, None, :], batch, seq)
            w_o = mlstm_w_o[j]
        xf = _proj_norm_res(a, w_o.astype(BF16), gain(norm_mix_post[i]), xf)

        kv = _norm_matmul(memf, gain(norm_mem_kv[i]), mem_w_kv[i].astype(BF16))
        xf = _mem_xattn(xf, gain(norm_mem_q[i]), mem_w_q[i].astype(BF16), kv, mem_w_o[i].astype(BF16),
                        gain(norm_mem_post[i]), batch, seq)

        w_gu = ffn_w_gate_up[i]
        xf = _swiglu(xf, gain(norm_ffn_pre[i]), w_gu[:, :D_FF].astype(BF16), w_gu[:, D_FF:].astype(BF16),
                     ffn_w_down[i].astype(BF16), gain(norm_ffn_post[i]))
    return xf.reshape(batch, seq, d)
```

```python
import functools

import jax
import jax.numpy as jnp
from jax import lax
from jax.experimental import pallas as pl
from jax.experimental.pallas import tpu as pltpu

F32 = jnp.float32
BF16 = jnp.bfloat16

D_MODEL = 1024
DEPTH = 4
NORM_EPS = 1e-6

MLA_HEADS = 8
MLA_NOPE = 128
MLA_ROPE = 64
MLA_V = 128
MLA_Q_RANK = 384
MLA_KV_RANK = 256
MLA_QK_PAD = 256
ROPE_BASE = 10000.0

MLSTM_HEADS = 4
MLSTM_DV = 256
MLSTM_DK = 128
MLSTM_CHUNK = 256

N_MEM = 256
MEM_HEADS = 4
MEM_DH = 256

D_FF = 2816
FF_CHUNK = 256

ROW_TILE = 512
ATTN_TILE = 512
ATTN_HEAD_GROUP = 8
MLA_Q_SCALE = (MLA_NOPE + MLA_ROPE) ** -0.5 * 1.4426950408889634
LANE = 128
VMEM_LIMIT = 56 * 1024 * 1024


def _rms(x, g):
    ms = jnp.mean(x * x, axis=-1, keepdims=True)
    return x * lax.rsqrt(ms + NORM_EPS) * g


def _dot(a, b):
    return jnp.dot(a, b, preferred_element_type=F32)


def _dot_nt(a, b):
    return lax.dot_general(a, b, (((1,), (1,)), ((), ())), preferred_element_type=F32)


def _const_spec(shape):
    nd = len(shape)
    return pl.BlockSpec(shape, lambda *_: (0,) * nd, pipeline_mode=pl.Buffered(1))


def _params(*sem):
    return pltpu.CompilerParams(dimension_semantics=sem, vmem_limit_bytes=VMEM_LIMIT)


def _rope_table_kernel(pos_ref, freq_ref, cmask_ref, smask_ref, cos_t_ref, sin_t_ref, cos_ref, sin_ref):
    ang = freq_ref[...] * pos_ref[...].astype(F32)
    cos_t = jnp.cos(ang) * cmask_ref[...]
    sin_t = jnp.sin(ang) * smask_ref[...]
    cos_t_ref[...] = cos_t
    sin_t_ref[...] = sin_t
    cos_ref[...] = cos_t.T
    sin_ref[...] = sin_t.T


def _rope_tables(pos_row):
    t = pos_row.shape[1]
    half = MLA_ROPE // 2
    inv_freq = ROPE_BASE ** (-jnp.arange(0, MLA_ROPE, 2, dtype=F32) / MLA_ROPE)
    zeros = jnp.zeros((LANE - MLA_ROPE,), F32)
    freq = jnp.concatenate([inv_freq, inv_freq, zeros])[:, None]
    cmask = jnp.concatenate([jnp.ones((MLA_ROPE,), F32), zeros])[:, None]
    smask = jnp.concatenate([-jnp.ones((half,), F32), jnp.ones((half,), F32), zeros])[:, None]
    tm = 1024
    col = pl.BlockSpec((LANE, 1), lambda i: (0, 0))
    return pl.pallas_call(
        _rope_table_kernel,
        grid=(t // tm,),
        in_specs=[pl.BlockSpec((1, tm), lambda i: (0, i)), col, col, col],
        out_specs=[pl.BlockSpec((LANE, tm), lambda i: (0, i))] * 2 + [pl.BlockSpec((tm, LANE), lambda i: (i, 0))] * 2,
        out_shape=[jax.ShapeDtypeStruct((LANE, t), F32)] * 2 + [jax.ShapeDtypeStruct((t, LANE), F32)] * 2,
        compiler_params=_params("parallel"),
        name="rope_tables",
    )(pos_row, freq, cmask, smask)


def _norm_matmul_kernel(x_ref, g_ref, w_ref, o_ref):
    hn = _rms(x_ref[...], g_ref[...]).astype(BF16)
    o_ref[...] = _dot(hn, w_ref[...]).astype(o_ref.dtype)


def _norm_matmul(x, g, w, tm=ROW_TILE):
    t, d = x.shape
    n = w.shape[1]
    return pl.pallas_call(
        _norm_matmul_kernel,
        grid=(t // tm,),
        in_specs=[pl.BlockSpec((tm, d), lambda i: (i, 0)), _const_spec((1, d)), _const_spec((d, n))],
        out_specs=pl.BlockSpec((tm, n), lambda i: (i, 0)),
        out_shape=jax.ShapeDtypeStruct((t, n), BF16),
        compiler_params=_params("parallel"),
        name="norm_matmul",
    )(x, g, w)


def _mla_proj_kernel(x_ref, cos_t_ref, sin_t_ref, cos_ref, sin_ref, gpre_ref, win_ref, gq_ref, wqn_ref,
                     wqr_ref, wqs_ref, gkv_ref, wk_ref, wvt_ref, qt_ref, k_ref, vt_ref):
    hn = _rms(x_ref[...], gpre_ref[...]).astype(BF16)
    proj = _dot(hn, win_ref[...])
    cq = proj[:, :MLA_Q_RANK]
    ckv = proj[:, MLA_Q_RANK:MLA_Q_RANK + MLA_KV_RANK]
    kr = proj[:, MLA_Q_RANK + MLA_KV_RANK:MLA_Q_RANK + MLA_KV_RANK + LANE]
    krs = proj[:, MLA_Q_RANK + MLA_KV_RANK + LANE:]
    k_rope = (kr * cos_ref[...] + krs * sin_ref[...]).astype(BF16)

    cqn = _rms(cq, gq_ref[...]).astype(BF16)
    q_nope = _dot_nt(wqn_ref[...], cqn)
    q_r = _dot_nt(wqr_ref[...], cqn)
    q_s = _dot_nt(wqs_ref[...], cqn)
    ckvn = _rms(ckv, gkv_ref[...]).astype(BF16)
    k_nope = _dot(ckvn, wk_ref[...])
    vt_ref[0] = _dot_nt(wvt_ref[...], ckvn).astype(BF16)

    cos_t = cos_t_ref[...] * MLA_Q_SCALE
    sin_t = sin_t_ref[...] * MLA_Q_SCALE
    for h in range(MLA_HEADS):
        lo, hi = h * LANE, (h + 1) * LANE
        base = h * MLA_QK_PAD
        qt_ref[0, base:base + LANE, :] = (q_nope[lo:hi] * MLA_Q_SCALE).astype(BF16)
        qt_ref[0, base + LANE:base + 2 * LANE, :] = (q_r[lo:hi] * cos_t + q_s[lo:hi] * sin_t).astype(BF16)
        k_ref[:, base:base + LANE] = k_nope[:, lo:hi].astype(BF16)
        k_ref[:, base + LANE:base + 2 * LANE] = k_rope


def _mla_proj(x, tables, gpre, win, gq, wqn, wqr, wqs, gkv, wk, wvt, tm=ATTN_TILE):
    t, d = x.shape
    hq = MLA_HEADS * MLA_QK_PAD
    hv = MLA_HEADS * MLA_V
    rows = lambda n: pl.BlockSpec((tm, n), lambda i: (i, 0))
    cols = lambda n: pl.BlockSpec((n, tm), lambda i: (0, i))
    tiles = lambda n: pl.BlockSpec((1, n, tm), lambda i: (i, 0, 0))
    return pl.pallas_call(
        _mla_proj_kernel,
        grid=(t // tm,),
        in_specs=[rows(d), cols(LANE), cols(LANE), rows(LANE), rows(LANE), _const_spec(gpre.shape),
                  _const_spec(win.shape), _const_spec(gq.shape), _const_spec(wqn.shape),
                  _const_spec(wqr.shape), _const_spec(wqs.shape), _const_spec(gkv.shape),
                  _const_spec(wk.shape), _const_spec(wvt.shape)],
        out_specs=[tiles(hq), rows(hq), tiles(hv)],
        out_shape=[jax.ShapeDtypeStruct((t // tm, hq, tm), BF16), jax.ShapeDtypeStruct((t, hq), BF16),
                   jax.ShapeDtypeStruct((t // tm, hv, tm), BF16)],
        compiler_params=_params("parallel"),
        name="mla_proj",
    )(x, *tables, gpre, win, gq, wqn, wqr, wqs, gkv, wk, wvt)


def _mla_attn_kernel(q_ref, k_ref, vt_ref, o_ref, *, tile, group):
    i = pl.program_id(2)
    half = tile // 2
    chains = [(g, c) for g in range(group) for c in range(2)]
    qs = [q_ref[0, g * MLA_QK_PAD:(g + 1) * MLA_QK_PAD, c * half:(c + 1) * half] for g, c in chains]

    def scores(kb, q, mask=None):
        s = _dot(kb, q)
        return s if mask is None else jnp.where(mask, s, -jnp.inf)

    def accumulate(carry, s, vtb):
        m, l, acc = carry
        m_new = jnp.maximum(m, jnp.max(s, axis=0, keepdims=True))
        alpha = jnp.exp2(m - m_new)
        p = jnp.exp2(s - m_new)
        l = alpha * l + jnp.sum(p, axis=0, keepdims=True)
        acc = alpha * acc + _dot(vtb, p.astype(BF16))
        return m_new, l, acc

    def k_block(g, start, size):
        return k_ref[pl.ds(start, size), g * MLA_QK_PAD:(g + 1) * MLA_QK_PAD]

    def body(j, carries):
        start = pl.multiple_of(j * tile, tile)
        ss = [scores(k_block(g, start, tile), qs[n]) for n, (g, c) in enumerate(chains)]
        return tuple(accumulate(carries[n], ss[n], vt_ref[j, g * MLA_V:(g + 1) * MLA_V, :])
                     for n, (g, c) in enumerate(chains))

    init = (jnp.full((1, half), -jnp.inf, F32), jnp.zeros((1, half), F32), jnp.zeros((MLA_V, half), F32))
    carries = lax.fori_loop(0, i, body, (init,) * len(chains))

    base = pl.multiple_of(i * tile, tile)

    def causal(keys, shift):
        row = lax.broadcasted_iota(jnp.int32, (keys, half), 0)
        col = lax.broadcasted_iota(jnp.int32, (keys, half), 1)
        return row <= col + shift

    keys = [half, tile]
    ss = [scores(k_block(g, base, keys[c]), qs[n], causal(keys[c], c * half)) for n, (g, c) in enumerate(chains)]
    for n, (g, c) in enumerate(chains):
        _, l, acc = accumulate(carries[n], ss[n], vt_ref[i, g * MLA_V:(g + 1) * MLA_V, :keys[c]])
        o_ref[c * half:(c + 1) * half, g * MLA_V:(g + 1) * MLA_V] = (acc / l).T.astype(o_ref.dtype)


def _mla_attn(qt, k, vt, batch, seq, tile=ATTN_TILE, group=ATTN_HEAD_GROUP):
    t = k.shape[0]
    nq = seq // tile
    return pl.pallas_call(
        functools.partial(_mla_attn_kernel, tile=tile, group=group),
        grid=(batch, MLA_HEADS // group, nq),
        in_specs=[pl.BlockSpec((1, group * MLA_QK_PAD, tile), lambda b, h, i: (b * nq + i, h, 0)),
                  pl.BlockSpec((seq, group * MLA_QK_PAD), lambda b, h, i: (b, h)),
                  pl.BlockSpec((nq, group * MLA_V, tile), lambda b, h, i: (b, h, 0))],
        out_specs=pl.BlockSpec((tile, group * MLA_V), lambda b, h, i: (b * nq + i, h)),
        out_shape=jax.ShapeDtypeStruct((t, MLA_HEADS * MLA_V), BF16),
        compiler_params=_params("parallel", "parallel", "arbitrary"),
        name="mla_attn",
    )(qt, k, vt)


def _mlstm_proj_kernel(x_ref, gpre_ref, wk_ref, wt_ref, wg_ref, wgt_ref, bg_ref, bgt_ref,
                       k_ref, qt_ref, vt_ref, opt_ref, gc_ref, gr_ref):
    hn = _rms(x_ref[...], gpre_ref[...]).astype(BF16)
    nq = MLSTM_HEADS * MLSTM_DK
    nv = MLSTM_HEADS * MLSTM_DV
    k_ref[...] = _dot(hn, wk_ref[...]).astype(BF16)
    proj_t = _dot_nt(wt_ref[...], hn)
    qt_ref[...] = (proj_t[:nq] * MLSTM_DK ** -0.5).astype(BF16)
    vt_ref[...] = proj_t[nq:nq + nv].astype(BF16)
    opt_ref[...] = proj_t[nq + nv:]
    gc_ref[...] = _dot(hn, wg_ref[...]) + bg_ref[...]
    gr_ref[...] = _dot_nt(wgt_ref[...], hn) + bgt_ref[...]


def _mlstm_proj(x, gpre, wk, wt, wg, wgt, bg, bgt, tm=ROW_TILE):
    t, d = x.shape
    nq = MLSTM_HEADS * MLSTM_DK
    nv = MLSTM_HEADS * MLSTM_DV
    ng = 2 * MLSTM_HEADS
    rows = lambda n: pl.BlockSpec((tm, n), lambda i: (i, 0))
    cols = lambda n: pl.BlockSpec((n, tm), lambda i: (0, i))
    return pl.pallas_call(
        _mlstm_proj_kernel,
        grid=(t // tm,),
        in_specs=[rows(d), _const_spec(gpre.shape), _const_spec(wk.shape), _const_spec(wt.shape),
                  _const_spec(wg.shape), _const_spec(wgt.shape), _const_spec(bg.shape),
                  _const_spec(bgt.shape)],
        out_specs=[rows(nq), cols(nq), cols(nv), cols(nv), rows(LANE), cols(ng)],
        out_shape=[jax.ShapeDtypeStruct((t, nq), BF16), jax.ShapeDtypeStruct((nq, t), BF16),
                   jax.ShapeDtypeStruct((nv, t), BF16), jax.ShapeDtypeStruct((nv, t), F32),
                   jax.ShapeDtypeStruct((t, LANE), F32), jax.ShapeDtypeStruct((ng, t), F32)],
        compiler_params=_params("parallel"),
        name="mlstm_proj",
    )(x, gpre, wk, wt, wg, wgt, bg, bgt)


def _log_sigmoid(x):
    return jnp.minimum(x, 0.0) - jnp.log1p(jnp.exp(-jnp.abs(x)))


def _bf16_parts(x, axis):
    parts = []
    for _ in range(3):
        p = x.astype(BF16).astype(F32)
        parts.append(p)
        x = x - p
    return jnp.concatenate(parts, axis=axis).astype(BF16)


def _mlstm_scan_kernel(k_ref, qt_ref, vt_ref, opt_ref, gc_ref, gr_ref, hn_ref, o_ref, state_ref, m_ref):
    L = MLSTM_CHUNK
    H, DK, DV = MLSTM_HEADS, MLSTM_DK, MLSTM_DV

    @pl.when(pl.program_id(1) == 0)
    def _():
        state_ref[...] = jnp.zeros_like(state_ref)
        m_ref[...] = jnp.zeros_like(m_ref)

    gc = gc_ref[...]
    gr = gr_ref[...]
    row = lax.broadcasted_iota(jnp.int32, (L, L), 0)
    col = lax.broadcasted_iota(jnp.int32, (L, L), 1)
    causal_t = row <= col
    r = _dot((row >= col).astype(BF16), _bf16_parts(_log_sigmoid(gc), axis=1))
    b_cols = r[:, :LANE] + r[:, LANE:2 * LANE] + r[:, 2 * LANE:]
    r = _dot(_bf16_parts(_log_sigmoid(gr), axis=0), causal_t.astype(BF16))
    b_rows = r[:8] + r[8:16] + r[16:]

    ks = [k_ref[:, h * DK:(h + 1) * DK] for h in range(H)]
    qts = [qt_ref[h * DK:(h + 1) * DK, :] for h in range(H)]
    qk = [_dot(ks[h], qts[h]) for h in range(H)]
    inter = [_dot(state_ref[h].astype(BF16), qts[h]) for h in range(H)]

    for h in range(H):
        b_r = b_rows[H + h:H + h + 1, :]
        li_r = gr[h:h + 1, :]
        c_c = gc[:, h:h + 1] - b_cols[:, H + h:H + h + 1]
        m = m_ref[h][:, 0:1]
        d = jnp.where(causal_t, b_r + c_c, -jnp.inf)
        g = b_r + m
        mt = jnp.maximum(g, jnp.max(d, axis=0, keepdims=True))
        w_inter = jnp.exp(g - mt)
        s = qk[h] * jnp.exp(d - mt)
        vt = vt_ref[h * DV:(h + 1) * DV, :]
        num = w_inter * inter[h][:DV] + _dot(vt, s.astype(BF16))
        den = w_inter * inter[h][DV:DV + 1] + jnp.sum(s, axis=0, keepdims=True)
        r_den = 1.0 / jnp.maximum(jnp.abs(den), jnp.exp(-mt))
        ms = jnp.mean(num * num, axis=0, keepdims=True) * (r_den * r_den)
        hcn = num * (r_den * lax.rsqrt(ms + NORM_EPS)) * hn_ref[h]
        out = jax.nn.sigmoid(opt_ref[h * DV:(h + 1) * DV, :]) * hcn
        o_ref[:, h * DV:(h + 1) * DV] = out.T.astype(o_ref.dtype)

        b_last = b_r[:, L - 1:L]
        m_new = mt[:, L - 1:L]
        w_k = jnp.exp(b_last - b_r + li_r - m_new)
        decay = jnp.exp(b_last + m - m_new)
        vt_ext = jnp.concatenate([(vt.astype(F32) * w_k).astype(BF16),
                                  jnp.broadcast_to(w_k, (8, L)).astype(BF16)], axis=0)
        state_ref[h] = decay * state_ref[h] + _dot(vt_ext, ks[h])
        m_ref[h] = jnp.broadcast_to(m_new, (1, LANE))


def _mlstm_scan(k, qt, vt, opt, gc, gr, hnorm, batch, seq):
    t = k.shape[0]
    L = MLSTM_CHUNK
    nc = seq // L
    nq = MLSTM_HEADS * MLSTM_DK
    nv = MLSTM_HEADS * MLSTM_DV
    rows = lambda n: pl.BlockSpec((L, n), lambda b, c: (b * nc + c, 0))
    cols = lambda n: pl.BlockSpec((n, L), lambda b, c: (0, b * nc + c))
    return pl.pallas_call(
        _mlstm_scan_kernel,
        grid=(batch, nc),
        in_specs=[rows(nq), cols(nq), cols(nv), cols(nv), rows(LANE), cols(2 * MLSTM_HEADS),
                  _const_spec(hnorm.shape)],
        out_specs=rows(nv),
        out_shape=jax.ShapeDtypeStruct((t, nv), BF16),
        scratch_shapes=[pltpu.VMEM((MLSTM_HEADS, MLSTM_DV + 8, MLSTM_DK), F32),
                        pltpu.VMEM((MLSTM_HEADS, 1, LANE), F32)],
        compiler_params=_params("parallel", "arbitrary"),
        name="mlstm_scan",
    )(k, qt, vt, opt, gc, gr, hnorm)


def _mix_out_mem_xattn_kernel(a_ref, x_ref, wmix_ref, gmix_ref, gq_ref, wq_ref, kv_ref, wo_ref, gpost_ref,
                              o_ref, oc_ref):
    x = x_ref[...] + _rms(_dot(a_ref[...], wmix_ref[...]), gmix_ref[...])
    hn = _rms(x, gq_ref[...]).astype(BF16)
    q = (_dot(hn, wq_ref[...]) * MEM_DH ** -0.5).astype(BF16)
    nk = MEM_HEADS * MEM_DH
    heads = [(h * MEM_DH, (h + 1) * MEM_DH) for h in range(MEM_HEADS)]
    scores = [_dot_nt(q[:, lo:hi], kv_ref[:, lo:hi]) for lo, hi in heads]
    for (lo, hi), s in zip(heads, scores):
        e = jnp.exp(s - jnp.max(s, axis=-1, keepdims=True))
        p = e / jnp.sum(e, axis=-1, keepdims=True)
        oc_ref[:, lo:hi] = _dot(p.astype(BF16), kv_ref[:, nk + lo:nk + hi]).astype(BF16)
    o_ref[...] = x + _rms(_dot(oc_ref[...], wo_ref[...]), gpost_ref[...])


def _mix_out_mem_xattn(a, x, wmix, gmix, gq, wq, kv, wo, gpost, batch, seq, tm=ROW_TILE):
    t, d = x.shape
    nt = seq // tm
    rows = lambda n: pl.BlockSpec((tm, n), lambda b, i: (b * nt + i, 0))
    return pl.pallas_call(
        _mix_out_mem_xattn_kernel,
        grid=(batch, nt),
        in_specs=[rows(a.shape[1]), rows(d), _const_spec(wmix.shape), _const_spec(gmix.shape),
                  _const_spec(gq.shape), _const_spec(wq.shape),
                  pl.BlockSpec((N_MEM, kv.shape[1]), lambda b, i: (b, 0)),
                  _const_spec(wo.shape), _const_spec(gpost.shape)],
        out_specs=rows(d),
        out_shape=jax.ShapeDtypeStruct((t, d), F32),
        scratch_shapes=[pltpu.VMEM((tm, MEM_HEADS * MEM_DH), BF16)],
        compiler_params=_params("parallel", "parallel"),
        name="mix_out_mem_xattn",
    )(a, x, wmix, gmix, gq, wq, kv, wo, gpost)


def _swiglu_kernel(x_ref, gpre_ref, wg_ref, wu_ref, wd_ref, gpost_ref, o_ref, act_ref):
    x = x_ref[...]
    hn = _rms(x, gpre_ref[...]).astype(BF16)
    for c in range(D_FF // FF_CHUNK):
        lo, hi = c * FF_CHUNK, (c + 1) * FF_CHUNK
        gate = _dot(hn, wg_ref[:, lo:hi])
        up = _dot(hn, wu_ref[:, lo:hi])
        act_ref[:, lo:hi] = (gate * jax.nn.sigmoid(gate) * up).astype(BF16)
    o_ref[...] = x + _rms(_dot(act_ref[...], wd_ref[...]), gpost_ref[...])


def _swiglu(x, gpre, wg, wu, wd, gpost, tm=ROW_TILE):
    t, d = x.shape
    return pl.pallas_call(
        _swiglu_kernel,
        grid=(t // tm,),
        in_specs=[pl.BlockSpec((tm, d), lambda i: (i, 0)), _const_spec(gpre.shape), _const_spec(wg.shape),
                  _const_spec(wu.shape), _const_spec(wd.shape), _const_spec(gpost.shape)],
        out_specs=pl.BlockSpec((tm, d), lambda i: (i, 0)),
        out_shape=jax.ShapeDtypeStruct((t, d), F32),
        scratch_shapes=[pltpu.VMEM((tm, D_FF), BF16)],
        compiler_params=_params("parallel"),
        name="swiglu",
    )(x, gpre, wg, wu, wd, gpost)


def _pad_cols(w, n):
    return jnp.pad(w, ((0, 0), (0, n - w.shape[1])))


def _swap_halves(w):
    half = w.shape[-1] // 2
    return jnp.concatenate([w[..., half:], w[..., :half]], axis=-1)


def _mla_weights(w_in, w_uq, w_ukv):
    r = MLA_Q_RANK + MLA_KV_RANK
    w_kr = w_in[:, r:]
    win = jnp.concatenate([w_in[:, :r], _pad_cols(w_kr, LANE), _pad_cols(_swap_halves(w_kr), LANE)], axis=1)
    uq = w_uq.reshape(MLA_Q_RANK, MLA_HEADS, MLA_NOPE + MLA_ROPE)
    wqn = uq[:, :, :MLA_NOPE].reshape(MLA_Q_RANK, MLA_HEADS * LANE)
    rope = uq[:, :, MLA_NOPE:]
    pad = ((0, 0), (0, 0), (0, LANE - MLA_ROPE))
    wqr = jnp.pad(rope, pad).reshape(MLA_Q_RANK, MLA_HEADS * LANE)
    wqs = jnp.pad(_swap_halves(rope), pad).reshape(MLA_Q_RANK, MLA_HEADS * LANE)
    ukv = w_ukv.reshape(MLA_KV_RANK, MLA_HEADS, 2, MLA_NOPE)
    wk = ukv[:, :, 0, :].reshape(MLA_KV_RANK, MLA_HEADS * MLA_NOPE)
    wvt = ukv[:, :, 1, :].reshape(MLA_KV_RANK, MLA_HEADS * MLA_V).T
    return tuple(w.astype(BF16) for w in (win, wqn.T, wqr.T, wqs.T, wk, wvt))


def _mlstm_weights(w_in, b_gates):
    nq = MLSTM_HEADS * MLSTM_DK
    nv = MLSTM_HEADS * MLSTM_DV
    w_q, w_k = w_in[:, :nq], w_in[:, nq:2 * nq]
    w_vo = w_in[:, 2 * nq:2 * nq + 2 * nv]
    w_g = w_in[:, 2 * nq + 2 * nv:]
    w_t = jnp.concatenate([w_q, w_vo], axis=1).T.astype(BF16)
    bg = _pad_cols(b_gates[None, :], LANE)
    return (w_k.astype(BF16), w_t, _pad_cols(w_g, LANE).astype(BF16), w_g.T.astype(BF16),
            bg, b_gates[:, None])


def kernel(x, mem, positions, mla_w_in, mla_q_norm, mla_w_uq, mla_kv_norm, mla_w_ukv, mla_w_o, mlstm_w_in, mlstm_b_gates, mlstm_head_norm, mlstm_w_o, norm_mix_pre, norm_mix_post, norm_mem_q, norm_mem_kv, norm_mem_post, norm_ffn_pre, norm_ffn_post, mem_w_q, mem_w_kv, mem_w_o, ffn_w_gate_up, ffn_w_down):
    batch, seq, d = x.shape
    t = batch * seq
    xf = x.reshape(t, d)
    memf = mem.reshape(batch * mem.shape[1], d)
    tables = _rope_tables(positions.reshape(1, t))
    gain = lambda g: g[None, :]

    for i in range(DEPTH):
        j = i // 2
        if i % 2 == 0:
            win, wqn, wqr, wqs, wk, wvt = _mla_weights(mla_w_in[j], mla_w_uq[j], mla_w_ukv[j])
            qt, k, vt = _mla_proj(xf, tables, gain(norm_mix_pre[i]), win, gain(mla_q_norm[j]),
                                  wqn, wqr, wqs, gain(mla_kv_norm[j]), wk, wvt)
            a = _mla_attn(qt, k, vt, batch, seq)
            w_o = mla_w_o[j]
        else:
            wk, wt, wg, wgt, bg, bgt = _mlstm_weights(mlstm_w_in[j], mlstm_b_gates[j])
            k, qt, vt, opt, gc, gr = _mlstm_proj(xf, gain(norm_mix_pre[i]), wk, wt, wg, wgt, bg, bgt)
            a = _mlstm_scan(k, qt, vt, opt, gc, gr, mlstm_head_norm[j][:, :, None], batch, seq)
            w_o = mlstm_w_o[j]

        kv = _norm_matmul(memf, gain(norm_mem_kv[i]), mem_w_kv[i].astype(BF16))
        xf = _mix_out_mem_xattn(a, xf, w_o.astype(BF16), gain(norm_mix_post[i]), gain(norm_mem_q[i]),
                                mem_w_q[i].astype(BF16), kv, mem_w_o[i].astype(BF16),
                                gain(norm_mem_post[i]), batch, seq)

        w_gu = ffn_w_gate_up[i]
        xf = _swiglu(xf, gain(norm_ffn_pre[i]), w_gu[:, :D_FF].astype(BF16), w_gu[:, D_FF:].astype(BF16),
                     ffn_w_down[i].astype(BF16), gain(norm_ffn_post[i]))
    return xf.reshape(batch, seq, d)
```

```python
import functools

import jax
import jax.numpy as jnp
from jax import lax
from jax.experimental import pallas as pl
from jax.experimental.pallas import tpu as pltpu

F32 = jnp.float32
BF16 = jnp.bfloat16

D_MODEL = 1024
DEPTH = 4
NORM_EPS = 1e-6

MLA_HEADS = 8
MLA_NOPE = 128
MLA_ROPE = 64
MLA_V = 128
MLA_Q_RANK = 384
MLA_KV_RANK = 256
MLA_QK_PAD = 256
ROPE_BASE = 10000.0

MLSTM_HEADS = 4
MLSTM_DV = 256
MLSTM_DK = 128
MLSTM_CHUNK = 256

N_MEM = 256
MEM_HEADS = 4
MEM_DH = 256

D_FF = 2816
FF_CHUNK = 256

ROW_TILE = 512
SUB_TILES = 2
ATTN_TILE = 512
MLA_Q_SCALE = (MLA_NOPE + MLA_ROPE) ** -0.5 * 1.4426950408889634
LANE = 128
VMEM_LIMIT = 56 * 1024 * 1024


def _rms(x, g):
    ms = jnp.mean(x * x, axis=-1, keepdims=True)
    return x * lax.rsqrt(ms + NORM_EPS) * g


def _dot(a, b):
    return jnp.dot(a, b, preferred_element_type=F32)


def _dot_nt(a, b):
    return lax.dot_general(a, b, (((1,), (1,)), ((), ())), preferred_element_type=F32)


def _layer_spec(w, layer):
    nd = w.ndim - 1
    return pl.BlockSpec((None,) + w.shape[1:], lambda *_: (layer,) + (0,) * nd, pipeline_mode=pl.Buffered(1))


def _split(params):
    return [w for w, _ in params], [_layer_spec(w, layer) for w, layer in params]


def _sub_tiles(rows):
    return [pl.ds(i * (rows // SUB_TILES), rows // SUB_TILES) for i in range(SUB_TILES)]


def _params(*sem):
    return pltpu.CompilerParams(dimension_semantics=sem, vmem_limit_bytes=VMEM_LIMIT)


def _rope_table_kernel(pos_ref, freq_ref, cos_t_ref, sin_t_ref, cos_ref, sin_ref):
    ang = freq_ref[...] * pos_ref[...].astype(F32)
    c, s = jnp.cos(ang), jnp.sin(ang)
    zeros = jnp.zeros((LANE - MLA_ROPE, ang.shape[1]), F32)
    cos_t = jnp.concatenate([c, c, zeros], axis=0)
    sin_t = jnp.concatenate([-s, s, zeros], axis=0)
    cos_t_ref[...] = cos_t
    sin_t_ref[...] = sin_t
    cos_ref[...] = cos_t.T
    sin_ref[...] = sin_t.T


def _rope_tables(pos_row):
    t = pos_row.shape[1]
    half = MLA_ROPE // 2
    inv_freq = ROPE_BASE ** (-jnp.arange(0, MLA_ROPE, 2, dtype=F32) / MLA_ROPE)
    tm = 1024
    return pl.pallas_call(
        _rope_table_kernel,
        grid=(t // tm,),
        in_specs=[pl.BlockSpec((1, tm), lambda i: (0, i)), pl.BlockSpec((half, 1), lambda i: (0, 0))],
        out_specs=[pl.BlockSpec((LANE, tm), lambda i: (0, i))] * 2 + [pl.BlockSpec((tm, LANE), lambda i: (i, 0))] * 2,
        out_shape=[jax.ShapeDtypeStruct((LANE, t), F32)] * 2 + [jax.ShapeDtypeStruct((t, LANE), F32)] * 2,
        compiler_params=_params("parallel"),
        name="rope_tables",
    )(pos_row, inv_freq[:, None])


def _norm_matmul_kernel(x_ref, g_ref, w_ref, o_ref):
    hn = _rms(x_ref[...], g_ref[...]).astype(BF16)
    o_ref[...] = _dot(hn, w_ref[...]).astype(o_ref.dtype)


def _norm_matmul(x, params, tm=ROW_TILE):
    t, d = x.shape
    arrays, specs = _split(params)
    n = arrays[-1].shape[-1]
    return pl.pallas_call(
        _norm_matmul_kernel,
        grid=(t // tm,),
        in_specs=[pl.BlockSpec((tm, d), lambda i: (i, 0))] + specs,
        out_specs=pl.BlockSpec((tm, n), lambda i: (i, 0)),
        out_shape=jax.ShapeDtypeStruct((t, n), BF16),
        compiler_params=_params("parallel"),
        name="norm_matmul",
    )(x, *arrays)


def _mla_proj_kernel(x_ref, cos_t_ref, sin_t_ref, cos_ref, sin_ref, gpre_ref, win_ref, gq_ref, wqn_ref,
                     wqr_ref, wqs_ref, gkv_ref, wk_ref, wvt_ref, qt_ref, k_ref, vt_ref):
    subs = _sub_tiles(x_ref.shape[0])
    hns = [_rms(x_ref[r, :], gpre_ref[...]).astype(BF16) for r in subs]
    projs = [_dot(hn, win_ref[...]) for hn in hns]
    for r, proj in zip(subs, projs):
        cq = proj[:, :MLA_Q_RANK]
        ckv = proj[:, MLA_Q_RANK:MLA_Q_RANK + MLA_KV_RANK]
        kr = proj[:, MLA_Q_RANK + MLA_KV_RANK:MLA_Q_RANK + MLA_KV_RANK + LANE]
        krs = proj[:, MLA_Q_RANK + MLA_KV_RANK + LANE:]
        k_rope = (kr * cos_ref[r, :] + krs * sin_ref[r, :]).astype(BF16)

        cqn = _rms(cq, gq_ref[...]).astype(BF16)
        q_nope = _dot_nt(wqn_ref[...], cqn)
        q_r = _dot_nt(wqr_ref[...], cqn)
        q_s = _dot_nt(wqs_ref[...], cqn)
        ckvn = _rms(ckv, gkv_ref[...]).astype(BF16)
        k_nope = _dot(ckvn, wk_ref[...])
        vt_ref[0, :, r] = _dot_nt(wvt_ref[...], ckvn).astype(BF16)

        cos_t = cos_t_ref[:, r] * MLA_Q_SCALE
        sin_t = sin_t_ref[:, r] * MLA_Q_SCALE
        for h in range(MLA_HEADS):
            lo, hi = h * LANE, (h + 1) * LANE
            base = h * MLA_QK_PAD
            qt_ref[0, base:base + LANE, r] = (q_nope[lo:hi] * MLA_Q_SCALE).astype(BF16)
            qt_ref[0, base + LANE:base + 2 * LANE, r] = (q_r[lo:hi] * cos_t + q_s[lo:hi] * sin_t).astype(BF16)
            k_ref[r, base:base + LANE] = k_nope[:, lo:hi].astype(BF16)
            k_ref[r, base + LANE:base + 2 * LANE] = k_rope


def _mla_proj(x, tables, params, tm=ATTN_TILE):
    t, d = x.shape
    hq = MLA_HEADS * MLA_QK_PAD
    hv = MLA_HEADS * MLA_V
    arrays, specs = _split(params)
    rows = lambda n: pl.BlockSpec((tm, n), lambda i: (i, 0))
    cols = lambda n: pl.BlockSpec((n, tm), lambda i: (0, i))
    tiles = lambda n: pl.BlockSpec((1, n, tm), lambda i: (i, 0, 0))
    return pl.pallas_call(
        _mla_proj_kernel,
        grid=(t // tm,),
        in_specs=[rows(d), cols(LANE), cols(LANE), rows(LANE), rows(LANE)] + specs,
        out_specs=[tiles(hq), rows(hq), tiles(hv)],
        out_shape=[jax.ShapeDtypeStruct((t // tm, hq, tm), BF16), jax.ShapeDtypeStruct((t, hq), BF16),
                   jax.ShapeDtypeStruct((t // tm, hv, tm), BF16)],
        compiler_params=_params("parallel"),
        name="mla_proj",
    )(x, *tables, *arrays)


def _mla_attn_kernel(q_ref, k_ref, vt_ref, *rest, tile, n_full):
    o_ref = rest[-1]
    half = tile // 2
    chains = [(g, c) for g in range(MLA_HEADS) for c in range(2)]
    qs = [q_ref[0, g * MLA_QK_PAD:(g + 1) * MLA_QK_PAD, c * half:(c + 1) * half] for g, c in chains]

    def scores(kb, q, mask=None):
        s = _dot(kb, q)
        return s if mask is None else jnp.where(mask, s, -jnp.inf)

    def accumulate(carry, s, vtb):
        m, l, acc = carry
        m_new = jnp.maximum(m, jnp.max(s, axis=0, keepdims=True))
        alpha = jnp.exp2(m - m_new)
        p = jnp.exp2(s - m_new)
        l = alpha * l + jnp.sum(p, axis=0, keepdims=True)
        acc = alpha * acc + _dot(vtb, p.astype(BF16))
        return m_new, l, acc

    def k_block(g, start, size):
        return k_ref[start:start + size, g * MLA_QK_PAD:(g + 1) * MLA_QK_PAD]

    init = (jnp.full((1, half), -jnp.inf, F32), jnp.zeros((1, half), F32), jnp.zeros((MLA_V, half), F32))
    carries = [init] * len(chains)
    for j in range(n_full):
        ss = [scores(k_block(g, j * tile, tile), qs[n]) for n, (g, c) in enumerate(chains)]
        carries = [accumulate(carries[n], ss[n], vt_ref[j, g * MLA_V:(g + 1) * MLA_V, :])
                   for n, (g, c) in enumerate(chains)]

    def causal(keys, shift):
        row = lax.broadcasted_iota(jnp.int32, (keys, half), 0)
        col = lax.broadcasted_iota(jnp.int32, (keys, half), 1)
        return row <= col + shift

    keys = [half, tile]
    ss = [scores(k_block(g, n_full * tile, keys[c]), qs[n], causal(keys[c], c * half))
          for n, (g, c) in enumerate(chains)]
    for n, (g, c) in enumerate(chains):
        _, l, acc = accumulate(carries[n], ss[n], vt_ref[n_full, g * MLA_V:(g + 1) * MLA_V, :keys[c]])
        o_ref[c * half:(c + 1) * half, g * MLA_V:(g + 1) * MLA_V] = (acc / l).T.astype(o_ref.dtype)


def _mla_attn(qt, k, vt, batch, seq, tile=ATTN_TILE):
    t = k.shape[0]
    nq = seq // tile
    out = None
    for i in range(nq):
        kv_tiles = i + 1 if nq % (i + 1) == 0 else nq
        per_batch = nq // kv_tiles
        in_specs = [pl.BlockSpec((1, MLA_HEADS * MLA_QK_PAD, tile), lambda b, i=i: (b * nq + i, 0, 0)),
                    pl.BlockSpec((kv_tiles * tile, MLA_HEADS * MLA_QK_PAD), lambda b, n=per_batch: (b * n, 0)),
                    pl.BlockSpec((kv_tiles, MLA_HEADS * MLA_V, tile), lambda b, n=per_batch: (b * n, 0, 0))]
        operands = [qt, k, vt]
        if out is not None:
            in_specs.append(pl.BlockSpec(memory_space=pl.ANY))
            operands.append(out)
        out = pl.pallas_call(
            functools.partial(_mla_attn_kernel, tile=tile, n_full=i),
            grid=(batch,),
            in_specs=in_specs,
            out_specs=pl.BlockSpec((tile, MLA_HEADS * MLA_V), lambda b, i=i: (b * nq + i, 0)),
            out_shape=jax.ShapeDtypeStruct((t, MLA_HEADS * MLA_V), BF16),
            input_output_aliases={3: 0} if len(operands) == 4 else {},
            compiler_params=_params("parallel"),
            name=f"mla_attn_q{i}",
        )(*operands)
    return out


def _mlstm_proj_kernel(x_ref, gpre_ref, wk_ref, wt_ref, wg_ref, wgt_ref, bg_ref, bgt_ref,
                       k_ref, qt_ref, vt_ref, sgt_ref, gc_ref, gr_ref):
    nq = MLSTM_HEADS * MLSTM_DK
    nv = MLSTM_HEADS * MLSTM_DV
    subs = _sub_tiles(x_ref.shape[0])
    hns = [_rms(x_ref[r, :], gpre_ref[...]).astype(BF16) for r in subs]
    projs = [_dot_nt(wt_ref[...], hn) for hn in hns]
    for r, hn, proj_t in zip(subs, hns, projs):
        k_ref[r, :] = _dot(hn, wk_ref[...]).astype(BF16)
        gc_ref[r, :] = _dot(hn, wg_ref[...]) + bg_ref[...]
        gr_ref[:, r] = _dot_nt(wgt_ref[...], hn) + bgt_ref[...]
        qt_ref[:, r] = (proj_t[:nq] * MLSTM_DK ** -0.5).astype(BF16)
        vt_ref[:, r] = proj_t[nq:nq + nv].astype(BF16)
        sgt_ref[:, r] = jax.nn.sigmoid(proj_t[nq + nv:])


def _mlstm_proj(x, params, tm=ROW_TILE):
    t, d = x.shape
    nq = MLSTM_HEADS * MLSTM_DK
    nv = MLSTM_HEADS * MLSTM_DV
    ng = 2 * MLSTM_HEADS
    arrays, specs = _split(params)
    rows = lambda n: pl.BlockSpec((tm, n), lambda i: (i, 0))
    cols = lambda n: pl.BlockSpec((n, tm), lambda i: (0, i))
    return pl.pallas_call(
        _mlstm_proj_kernel,
        grid=(t // tm,),
        in_specs=[rows(d)] + specs,
        out_specs=[rows(nq), cols(nq), cols(nv), cols(nv), rows(LANE), cols(ng)],
        out_shape=[jax.ShapeDtypeStruct((t, nq), BF16), jax.ShapeDtypeStruct((nq, t), BF16),
                   jax.ShapeDtypeStruct((nv, t), BF16), jax.ShapeDtypeStruct((nv, t), F32),
                   jax.ShapeDtypeStruct((t, LANE), F32), jax.ShapeDtypeStruct((ng, t), F32)],
        compiler_params=_params("parallel"),
        name="mlstm_proj",
    )(x, *arrays)


def _log_sigmoid(x):
    return jnp.minimum(x, 0.0) - jnp.log1p(jnp.exp(-jnp.abs(x)))


def _bf16_parts(x, axis):
    parts = []
    for _ in range(3):
        p = x.astype(BF16).astype(F32)
        parts.append(p)
        x = x - p
    return jnp.concatenate(parts, axis=axis).astype(BF16)


def _mlstm_scan_kernel(k_ref, qt_ref, vt_ref, sgt_ref, gc_ref, gr_ref, hn_ref, o_ref, state_ref, m_ref):
    L = MLSTM_CHUNK
    H, DK, DV = MLSTM_HEADS, MLSTM_DK, MLSTM_DV

    @pl.when(pl.program_id(1) == 0)
    def _():
        state_ref[...] = jnp.zeros_like(state_ref)
        m_ref[...] = jnp.zeros_like(m_ref)

    gc = gc_ref[...]
    gr = gr_ref[...]
    row = lax.broadcasted_iota(jnp.int32, (L, L), 0)
    col = lax.broadcasted_iota(jnp.int32, (L, L), 1)
    causal_t = row <= col
    r = _dot((row >= col).astype(BF16), _bf16_parts(_log_sigmoid(gc), axis=1))
    b_cols = r[:, :LANE] + r[:, LANE:2 * LANE] + r[:, 2 * LANE:]
    r = _dot(_bf16_parts(_log_sigmoid(gr), axis=0), causal_t.astype(BF16))
    b_rows = r[:8] + r[8:16] + r[16:]

    ks = [k_ref[:, h * DK:(h + 1) * DK] for h in range(H)]
    qts = [qt_ref[h * DK:(h + 1) * DK, :] for h in range(H)]
    qk, inter, gates = [], [], []
    for h in range(H):
        qk.append(_dot(ks[h], qts[h]))
        inter.append(_dot(state_ref[h].astype(BF16), qts[h]))
        b_r = b_rows[H + h:H + h + 1, :]
        c_c = gc[:, h:h + 1] - b_cols[:, H + h:H + h + 1]
        m = m_ref[h][:, 0:1]
        d = jnp.where(causal_t, b_r + c_c, -jnp.inf)
        g = b_r + m
        mt = jnp.maximum(g, jnp.max(d, axis=0, keepdims=True))
        gates.append((b_r, m, mt, jnp.exp(g - mt), jnp.exp(d - mt)))

    for h in range(H):
        b_r, m, mt, w_inter, w_intra = gates[h]
        li_r = gr[h:h + 1, :]
        s = qk[h] * w_intra
        vt = vt_ref[h * DV:(h + 1) * DV, :]
        num = w_inter * inter[h][:DV] + _dot(vt, s.astype(BF16))
        den = w_inter * inter[h][DV:DV + 1] + jnp.sum(s, axis=0, keepdims=True)
        r_den = 1.0 / jnp.maximum(jnp.abs(den), jnp.exp(-mt))
        ms = jnp.mean(num * num, axis=0, keepdims=True) * (r_den * r_den)
        hcn = num * (r_den * lax.rsqrt(ms + NORM_EPS)) * hn_ref[h]
        out = sgt_ref[h * DV:(h + 1) * DV, :] * hcn
        o_ref[:, h * DV:(h + 1) * DV] = out.T.astype(o_ref.dtype)

        b_last = b_r[:, L - 1:L]
        m_new = mt[:, L - 1:L]
        w_k = jnp.exp(b_last - b_r + li_r - m_new)
        decay = jnp.exp(b_last + m - m_new)
        vt_ext = jnp.concatenate([(vt.astype(F32) * w_k).astype(BF16),
                                  jnp.broadcast_to(w_k, (8, L)).astype(BF16)], axis=0)
        state_ref[h] = decay * state_ref[h] + _dot(vt_ext, ks[h])
        m_ref[h] = jnp.broadcast_to(m_new, (1, LANE))


def _mlstm_scan(k, qt, vt, sgt, gc, gr, hnorm, batch, seq):
    t = k.shape[0]
    L = MLSTM_CHUNK
    nc = seq // L
    nq = MLSTM_HEADS * MLSTM_DK
    nv = MLSTM_HEADS * MLSTM_DV
    arrays, specs = _split([hnorm])
    rows = lambda n: pl.BlockSpec((L, n), lambda b, c: (b * nc + c, 0))
    cols = lambda n: pl.BlockSpec((n, L), lambda b, c: (0, b * nc + c))
    return pl.pallas_call(
        _mlstm_scan_kernel,
        grid=(batch, nc),
        in_specs=[rows(nq), cols(nq), cols(nv), cols(nv), rows(LANE), cols(2 * MLSTM_HEADS)] + specs,
        out_specs=rows(nv),
        out_shape=jax.ShapeDtypeStruct((t, nv), BF16),
        scratch_shapes=[pltpu.VMEM((MLSTM_HEADS, MLSTM_DV + 8, MLSTM_DK), F32),
                        pltpu.VMEM((MLSTM_HEADS, 1, LANE), F32)],
        compiler_params=_params("parallel", "arbitrary"),
        name="mlstm_scan",
    )(k, qt, vt, sgt, gc, gr, *arrays)


def _mix_out_mem_xattn_kernel(a_ref, x_ref, kv_ref, wmix_ref, gmix_ref, gq_ref, wq_ref, wo_ref, gpost_ref,
                              o_ref, oc_ref):
    subs = _sub_tiles(x_ref.shape[0])
    nk = MEM_HEADS * MEM_DH
    heads = [(h * MEM_DH, (h + 1) * MEM_DH) for h in range(MEM_HEADS)]
    mix = [_dot(a_ref[r, :], wmix_ref[...]) for r in subs]
    xs = [x_ref[r, :] + _rms(y, gmix_ref[...]) for r, y in zip(subs, mix)]
    hn = [_rms(x, gq_ref[...]).astype(BF16) for x in xs]
    qs = [(_dot(h, wq_ref[...]) * MEM_DH ** -0.5).astype(BF16) for h in hn]
    scores = [[_dot_nt(q[:, lo:hi], kv_ref[:, lo:hi]) for lo, hi in heads] for q in qs]
    for r, sub_scores in zip(subs, scores):
        for (lo, hi), s in zip(heads, sub_scores):
            e = jnp.exp(s - jnp.max(s, axis=-1, keepdims=True))
            p = e / jnp.sum(e, axis=-1, keepdims=True)
            oc_ref[r, lo:hi] = _dot(p.astype(BF16), kv_ref[:, nk + lo:nk + hi]).astype(BF16)
    outs = [_dot(oc_ref[r, :], wo_ref[...]) for r in subs]
    for r, x, y in zip(subs, xs, outs):
        o_ref[r, :] = x + _rms(y, gpost_ref[...])


def _mix_out_mem_xattn(a, x, kv, params, batch, seq, tm=ROW_TILE):
    t, d = x.shape
    nt = seq // tm
    arrays, specs = _split(params)
    rows = lambda n: pl.BlockSpec((tm, n), lambda b, i: (b * nt + i, 0))
    return pl.pallas_call(
        _mix_out_mem_xattn_kernel,
        grid=(batch, nt),
        in_specs=[rows(a.shape[1]), rows(d), pl.BlockSpec((N_MEM, kv.shape[1]), lambda b, i: (b, 0))] + specs,
        out_specs=rows(d),
        out_shape=jax.ShapeDtypeStruct((t, d), F32),
        scratch_shapes=[pltpu.VMEM((tm, MEM_HEADS * MEM_DH), BF16)],
        compiler_params=_params("parallel", "parallel"),
        name="mix_out_mem_xattn",
    )(a, x, kv, *arrays)


def _swiglu_kernel(x_ref, gpre_ref, wgu_ref, wd_ref, gpost_ref, o_ref, act_ref):
    subs = _sub_tiles(x_ref.shape[0])
    hn = [_rms(x_ref[r, :], gpre_ref[...]).astype(BF16) for r in subs]
    for c in range(D_FF // FF_CHUNK):
        lo, hi = c * FF_CHUNK, (c + 1) * FF_CHUNK
        for r, h in zip(subs, hn):
            gate = _dot(h, wgu_ref[:, lo:hi])
            up = _dot(h, wgu_ref[:, D_FF + lo:D_FF + hi])
            act_ref[r, lo:hi] = (gate * jax.nn.sigmoid(gate) * up).astype(BF16)
    down = [_dot(act_ref[r, :], wd_ref[...]) for r in subs]
    for r, y in zip(subs, down):
        o_ref[r, :] = x_ref[r, :] + _rms(y, gpost_ref[...])


def _swiglu(x, params, tm=ROW_TILE):
    t, d = x.shape
    arrays, specs = _split(params)
    return pl.pallas_call(
        _swiglu_kernel,
        grid=(t // tm,),
        in_specs=[pl.BlockSpec((tm, d), lambda i: (i, 0))] + specs,
        out_specs=pl.BlockSpec((tm, d), lambda i: (i, 0)),
        out_shape=jax.ShapeDtypeStruct((t, d), F32),
        scratch_shapes=[pltpu.VMEM((tm, D_FF), BF16)],
        compiler_params=_params("parallel"),
        name="swiglu",
    )(x, *arrays)


def _pad_last(w, n):
    return jnp.pad(w, [(0, 0)] * (w.ndim - 1) + [(0, n - w.shape[-1])])


def _swap_halves(w):
    half = w.shape[-1] // 2
    return jnp.concatenate([w[..., half:], w[..., :half]], axis=-1)


def _transpose(w):
    return jnp.swapaxes(w, -1, -2)


def _mla_weights(w_in, w_uq, w_ukv):
    n = w_in.shape[0]
    r = MLA_Q_RANK + MLA_KV_RANK
    w_kr = w_in[..., r:]
    win = jnp.concatenate([w_in[..., :r], _pad_last(w_kr, LANE), _pad_last(_swap_halves(w_kr), LANE)], axis=-1)
    uq = w_uq.reshape(n, MLA_Q_RANK, MLA_HEADS, MLA_NOPE + MLA_ROPE)
    wqn = uq[..., :MLA_NOPE].reshape(n, MLA_Q_RANK, MLA_HEADS * LANE)
    rope = uq[..., MLA_NOPE:]
    wqr = _pad_last(rope, LANE).reshape(n, MLA_Q_RANK, MLA_HEADS * LANE)
    wqs = _pad_last(_swap_halves(rope), LANE).reshape(n, MLA_Q_RANK, MLA_HEADS * LANE)
    ukv = w_ukv.reshape(n, MLA_KV_RANK, MLA_HEADS, 2, MLA_NOPE)
    wk = ukv[:, :, :, 0, :].reshape(n, MLA_KV_RANK, MLA_HEADS * MLA_NOPE)
    wvt = _transpose(ukv[:, :, :, 1, :].reshape(n, MLA_KV_RANK, MLA_HEADS * MLA_V))
    return tuple(w.astype(BF16) for w in (win, _transpose(wqn), _transpose(wqr), _transpose(wqs), wk, wvt))


def _mlstm_weights(w_in, b_gates):
    nq = MLSTM_HEADS * MLSTM_DK
    nv = MLSTM_HEADS * MLSTM_DV
    w_q, w_k = w_in[..., :nq], w_in[..., nq:2 * nq]
    w_vo = w_in[..., 2 * nq:2 * nq + 2 * nv]
    w_g = w_in[..., 2 * nq + 2 * nv:]
    w_t = _transpose(jnp.concatenate([w_q, w_vo], axis=-1)).astype(BF16)
    return (w_k.astype(BF16), w_t, _pad_last(w_g, LANE).astype(BF16), _transpose(w_g).astype(BF16),
            _pad_last(b_gates[:, None, :], LANE), b_gates[:, :, None])


def kernel(x, mem, positions, mla_w_in, mla_q_norm, mla_w_uq, mla_kv_norm, mla_w_ukv, mla_w_o, mlstm_w_in, mlstm_b_gates, mlstm_head_norm, mlstm_w_o, norm_mix_pre, norm_mix_post, norm_mem_q, norm_mem_kv, norm_mem_post, norm_ffn_pre, norm_ffn_post, mem_w_q, mem_w_kv, mem_w_o, ffn_w_gate_up, ffn_w_down):
    batch, seq, d = x.shape
    t = batch * seq
    xf = x.reshape(t, d)
    memf = mem.reshape(batch * mem.shape[1], d)
    tables = _rope_tables(positions.reshape(1, t))

    gain = lambda g: g[:, None, :]
    bf16 = lambda w: w.astype(BF16)
    g_mix_pre, g_mix_post = gain(norm_mix_pre), gain(norm_mix_post)
    g_mem_q, g_mem_kv, g_mem_post = gain(norm_mem_q), gain(norm_mem_kv), gain(norm_mem_post)
    g_ffn_pre, g_ffn_post = gain(norm_ffn_pre), gain(norm_ffn_post)
    mla_w = _mla_weights(mla_w_in, mla_w_uq, mla_w_ukv)
    mla_gq, mla_gkv, mla_wo = gain(mla_q_norm), gain(mla_kv_norm), bf16(mla_w_o)
    mlstm_w = _mlstm_weights(mlstm_w_in, mlstm_b_gates)
    mlstm_hn, mlstm_wo = mlstm_head_norm[..., None], bf16(mlstm_w_o)
    mem_wq, mem_wkv, mem_wo = bf16(mem_w_q), bf16(mem_w_kv), bf16(mem_w_o)
    ffn_wgu, ffn_wd = bf16(ffn_w_gate_up), bf16(ffn_w_down)

    for i in range(DEPTH):
        j = i // 2
        if i % 2 == 0:
            win, wqn, wqr, wqs, wk, wvt = mla_w
            qt, k, vt = _mla_proj(xf, tables, [(g_mix_pre, i), (win, j), (mla_gq, j), (wqn, j), (wqr, j),
                                               (wqs, j), (mla_gkv, j), (wk, j), (wvt, j)])
            a = _mla_attn(qt, k, vt, batch, seq)
            w_mix = mla_wo
        else:
            wk, wt, wg, wgt, bg, bgt = mlstm_w
            k, qt, vt, sgt, gc, gr = _mlstm_proj(xf, [(g_mix_pre, i), (wk, j), (wt, j), (wg, j), (wgt, j),
                                                      (bg, j), (bgt, j)])
            a = _mlstm_scan(k, qt, vt, sgt, gc, gr, (mlstm_hn, j), batch, seq)
            w_mix = mlstm_wo

        kv = _norm_matmul(memf, [(g_mem_kv, i), (mem_wkv, i)])
        xf = _mix_out_mem_xattn(a, xf, kv, [(w_mix, j), (g_mix_post, i), (g_mem_q, i), (mem_wq, i),
                                            (mem_wo, i), (g_mem_post, i)], batch, seq)
        xf = _swiglu(xf, [(g_ffn_pre, i), (ffn_wgu, i), (ffn_wd, i), (g_ffn_post, i)])
    return xf.reshape(batch, seq, d)
```

```python
import functools

import jax
import jax.numpy as jnp
from jax import lax
from jax.experimental import pallas as pl
from jax.experimental.pallas import tpu as pltpu

F32 = jnp.float32
BF16 = jnp.bfloat16

D_MODEL = 1024
DEPTH = 4
NORM_EPS = 1e-6

MLA_HEADS = 8
MLA_NOPE = 128
MLA_ROPE = 64
MLA_V = 128
MLA_Q_RANK = 384
MLA_KV_RANK = 256
MLA_QK_PAD = 256
ROPE_BASE = 10000.0

MLSTM_HEADS = 4
MLSTM_DV = 256
MLSTM_DK = 128
MLSTM_CHUNK = 256

N_MEM = 256
MEM_HEADS = 4
MEM_DH = 256

D_FF = 2816
FF_CHUNK = 256

ROW_TILE = 512
BIG_ROW_TILE = 1024
SUB_ROWS = 256
ATTN_TILE = 512
MLA_Q_SCALE = (MLA_NOPE + MLA_ROPE) ** -0.5 * 1.4426950408889634
LANE = 128
VMEM_LIMIT = 56 * 1024 * 1024


def _rms(x, g):
    ms = jnp.mean(x * x, axis=-1, keepdims=True)
    return x * lax.rsqrt(ms + NORM_EPS) * g


def _dot(a, b):
    return jnp.dot(a, b, preferred_element_type=F32)


def _dot_nt(a, b):
    return lax.dot_general(a, b, (((1,), (1,)), ((), ())), preferred_element_type=F32)


def _layer_spec(w, layer):
    nd = w.ndim - 1
    return pl.BlockSpec((None,) + w.shape[1:], lambda *_: (layer,) + (0,) * nd, pipeline_mode=pl.Buffered(1))


def _split(params):
    return [w for w, _ in params], [_layer_spec(w, layer) for w, layer in params]


def _sub_tiles(rows):
    return [pl.ds(i * SUB_ROWS, SUB_ROWS) for i in range(rows // SUB_ROWS)]


def _params(*sem):
    return pltpu.CompilerParams(dimension_semantics=sem, vmem_limit_bytes=VMEM_LIMIT)


def _rope_table_kernel(pos_ref, freq_ref, cos_t_ref, sin_t_ref, cos_ref, sin_ref):
    ang = freq_ref[...] * pos_ref[...].astype(F32)
    c, s = jnp.cos(ang), jnp.sin(ang)
    zeros = jnp.zeros((LANE - MLA_ROPE, ang.shape[1]), F32)
    cos_t = jnp.concatenate([c, c, zeros], axis=0)
    sin_t = jnp.concatenate([-s, s, zeros], axis=0)
    cos_t_ref[...] = cos_t
    sin_t_ref[...] = sin_t
    cos_ref[...] = cos_t.T
    sin_ref[...] = sin_t.T


def _rope_tables(pos_row):
    t = pos_row.shape[1]
    half = MLA_ROPE // 2
    inv_freq = ROPE_BASE ** (-jnp.arange(0, MLA_ROPE, 2, dtype=F32) / MLA_ROPE)
    tm = 1024
    return pl.pallas_call(
        _rope_table_kernel,
        grid=(t // tm,),
        in_specs=[pl.BlockSpec((1, tm), lambda i: (0, i)), pl.BlockSpec((half, 1), lambda i: (0, 0))],
        out_specs=[pl.BlockSpec((LANE, tm), lambda i: (0, i))] * 2 + [pl.BlockSpec((tm, LANE), lambda i: (i, 0))] * 2,
        out_shape=[jax.ShapeDtypeStruct((LANE, t), F32)] * 2 + [jax.ShapeDtypeStruct((t, LANE), F32)] * 2,
        compiler_params=_params("parallel"),
        name="rope_tables",
    )(pos_row, inv_freq[:, None])


def _norm_matmul_kernel(x_ref, g_ref, w_ref, o_ref):
    hn = _rms(x_ref[...], g_ref[...]).astype(BF16)
    o_ref[...] = _dot(hn, w_ref[...]).astype(o_ref.dtype)


def _norm_matmul(x, params, tm=ROW_TILE):
    t, d = x.shape
    arrays, specs = _split(params)
    n = arrays[-1].shape[-1]
    return pl.pallas_call(
        _norm_matmul_kernel,
        grid=(t // tm,),
        in_specs=[pl.BlockSpec((tm, d), lambda i: (i, 0))] + specs,
        out_specs=pl.BlockSpec((tm, n), lambda i: (i, 0)),
        out_shape=jax.ShapeDtypeStruct((t, n), BF16),
        compiler_params=_params("parallel"),
        name="norm_matmul",
    )(x, *arrays)


def _mla_proj_kernel(x_ref, cos_t_ref, sin_t_ref, cos_ref, sin_ref, gpre_ref, win_ref, gq_ref, wqn_ref,
                     wqr_ref, wqs_ref, gkv_ref, wk_ref, wvt_ref, qt_ref, k_ref, vt_ref):
    subs = _sub_tiles(x_ref.shape[0])
    hns = [_rms(x_ref[r, :], gpre_ref[...]).astype(BF16) for r in subs]
    projs = [_dot(hn, win_ref[...]) for hn in hns]
    for r, proj in zip(subs, projs):
        cq = proj[:, :MLA_Q_RANK]
        ckv = proj[:, MLA_Q_RANK:MLA_Q_RANK + MLA_KV_RANK]
        kr = proj[:, MLA_Q_RANK + MLA_KV_RANK:MLA_Q_RANK + MLA_KV_RANK + LANE]
        krs = proj[:, MLA_Q_RANK + MLA_KV_RANK + LANE:]
        k_rope = (kr * cos_ref[r, :] + krs * sin_ref[r, :]).astype(BF16)

        cqn = _rms(cq, gq_ref[...]).astype(BF16)
        q_nope = _dot_nt(wqn_ref[...], cqn)
        q_r = _dot_nt(wqr_ref[...], cqn)
        q_s = _dot_nt(wqs_ref[...], cqn)
        ckvn = _rms(ckv, gkv_ref[...]).astype(BF16)
        k_nope = _dot(ckvn, wk_ref[...])
        vt_ref[0, :, r] = _dot_nt(wvt_ref[...], ckvn).astype(BF16)

        cos_t = cos_t_ref[:MLA_ROPE, r] * MLA_Q_SCALE
        sin_t = sin_t_ref[:MLA_ROPE, r] * MLA_Q_SCALE
        zeros = jnp.zeros((MLA_QK_PAD - MLA_NOPE - MLA_ROPE, cos_t.shape[1]), BF16)
        for h in range(MLA_HEADS):
            lo, hi = h * LANE, (h + 1) * LANE
            rlo, rhi = h * MLA_ROPE, (h + 1) * MLA_ROPE
            base = h * MLA_QK_PAD
            qt_ref[0, base:base + MLA_NOPE, r] = (q_nope[lo:hi] * MLA_Q_SCALE).astype(BF16)
            qt_ref[0, base + MLA_NOPE:base + MLA_NOPE + MLA_ROPE, r] = (
                q_r[rlo:rhi] * cos_t + q_s[rlo:rhi] * sin_t).astype(BF16)
            qt_ref[0, base + MLA_NOPE + MLA_ROPE:base + MLA_QK_PAD, r] = zeros
            k_ref[r, base:base + LANE] = k_nope[:, lo:hi].astype(BF16)
            k_ref[r, base + LANE:base + 2 * LANE] = k_rope


def _mla_proj(x, tables, params, tm=ATTN_TILE):
    t, d = x.shape
    hq = MLA_HEADS * MLA_QK_PAD
    hv = MLA_HEADS * MLA_V
    arrays, specs = _split(params)
    rows = lambda n: pl.BlockSpec((tm, n), lambda i: (i, 0))
    cols = lambda n: pl.BlockSpec((n, tm), lambda i: (0, i))
    tiles = lambda n: pl.BlockSpec((1, n, tm), lambda i: (i, 0, 0))
    return pl.pallas_call(
        _mla_proj_kernel,
        grid=(t // tm,),
        in_specs=[rows(d), cols(LANE), cols(LANE), rows(LANE), rows(LANE)] + specs,
        out_specs=[tiles(hq), rows(hq), tiles(hv)],
        out_shape=[jax.ShapeDtypeStruct((t // tm, hq, tm), BF16), jax.ShapeDtypeStruct((t, hq), BF16),
                   jax.ShapeDtypeStruct((t // tm, hv, tm), BF16)],
        compiler_params=_params("parallel"),
        name="mla_proj",
    )(x, *tables, *arrays)


def _mla_attn_kernel(q_ref, k_ref, vt_ref, *rest, tile, n_full):
    o_ref = rest[-1]
    half = tile // 2
    chains = [(g, c) for g in range(MLA_HEADS) for c in range(2)]
    qs = [q_ref[0, g * MLA_QK_PAD:(g + 1) * MLA_QK_PAD, c * half:(c + 1) * half] for g, c in chains]

    def scores(kb, q, mask=None):
        s = _dot(kb, q)
        return s if mask is None else jnp.where(mask, s, -jnp.inf)

    def accumulate(carry, s, vtb):
        m, l, acc = carry
        m_new = jnp.maximum(m, jnp.max(s, axis=0, keepdims=True))
        alpha = jnp.exp2(m - m_new)
        p = jnp.exp2(s - m_new)
        l = alpha * l + jnp.sum(p, axis=0, keepdims=True)
        acc = alpha * acc + _dot(vtb, p.astype(BF16))
        return m_new, l, acc

    def k_block(g, start, size):
        return k_ref[start:start + size, g * MLA_QK_PAD:(g + 1) * MLA_QK_PAD]

    def causal(keys, shift):
        row = lax.broadcasted_iota(jnp.int32, (keys, half), 0)
        col = lax.broadcasted_iota(jnp.int32, (keys, half), 1)
        return row <= col + shift

    def block_scores(j, n):
        g, c = chains[n]
        if j < n_full:
            return scores(k_block(g, j * tile, tile), qs[n])
        keys = (half, tile)[c]
        return scores(k_block(g, j * tile, keys), qs[n], causal(keys, c * half))

    def block_vt(j, n):
        g, c = chains[n]
        keys = tile if j < n_full else (half, tile)[c]
        return vt_ref[j, g * MLA_V:(g + 1) * MLA_V, :keys]

    init = (jnp.full((1, half), -jnp.inf, F32), jnp.zeros((1, half), F32), jnp.zeros((MLA_V, half), F32))
    carries = [init] * len(chains)
    ss = [block_scores(0, n) for n in range(len(chains))]
    for j in range(n_full):
        for n in range(len(chains)):
            carries[n] = accumulate(carries[n], ss[n], block_vt(j, n))
            ss[n] = block_scores(j + 1, n)

    base = n_full * tile if o_ref.shape[0] > tile else 0
    if o_ref.shape[0] > tile:
        o_ref[...] = jnp.zeros_like(o_ref)
    for n, (g, c) in enumerate(chains):
        _, l, acc = accumulate(carries[n], ss[n], block_vt(n_full, n))
        o_ref[base + c * half:base + (c + 1) * half, g * MLA_V:(g + 1) * MLA_V] = (acc / l).T.astype(o_ref.dtype)


def _mla_attn(qt, k, vt, batch, seq, tile=ATTN_TILE):
    t = k.shape[0]
    nq = seq // tile
    out = None
    for i in reversed(range(nq)):
        kv_tiles = i + 1 if nq % (i + 1) == 0 else nq
        per_batch = nq // kv_tiles
        in_specs = [pl.BlockSpec((1, MLA_HEADS * MLA_QK_PAD, tile), lambda b, i=i: (b * nq + i, 0, 0)),
                    pl.BlockSpec((kv_tiles * tile, MLA_HEADS * MLA_QK_PAD), lambda b, n=per_batch: (b * n, 0)),
                    pl.BlockSpec((kv_tiles, MLA_HEADS * MLA_V, tile), lambda b, n=per_batch: (b * n, 0, 0))]
        operands = [qt, k, vt]
        if out is None:
            out_spec = pl.BlockSpec((seq, MLA_HEADS * MLA_V), lambda b: (b, 0))
        else:
            out_spec = pl.BlockSpec((tile, MLA_HEADS * MLA_V), lambda b, i=i: (b * nq + i, 0))
            in_specs.append(pl.BlockSpec(memory_space=pl.ANY))
            operands.append(out)
        out = pl.pallas_call(
            functools.partial(_mla_attn_kernel, tile=tile, n_full=i),
            grid=(batch,),
            in_specs=in_specs,
            out_specs=out_spec,
            out_shape=jax.ShapeDtypeStruct((t, MLA_HEADS * MLA_V), BF16),
            input_output_aliases={3: 0} if len(operands) == 4 else {},
            compiler_params=_params("parallel"),
            name=f"mla_attn_q{i}",
        )(*operands)
    return out


def _log_sigmoid(x):
    return jnp.minimum(x, 0.0) - jnp.log1p(jnp.exp(-jnp.abs(x)))


def _bf16_parts(x, axis):
    parts = []
    for _ in range(3):
        p = x.astype(BF16).astype(F32)
        parts.append(p)
        x = x - p
    return jnp.concatenate(parts, axis=axis).astype(BF16)


def _mlstm_proj_kernel(x_ref, gpre_ref, wk_ref, wt_ref, wgt_ref, bgt_ref, eye_ref,
                       k_ref, qt_ref, vt_ref, sgt_ref, gb_ref, cc_ref):
    nq = MLSTM_HEADS * MLSTM_DK
    nv = MLSTM_HEADS * MLSTM_DV
    L = MLSTM_CHUNK
    tri = (lax.broadcasted_iota(jnp.int32, (L, L), 0) <= lax.broadcasted_iota(jnp.int32, (L, L), 1)).astype(BF16)
    is_input_gate = lax.broadcasted_iota(jnp.int32, (2 * MLSTM_HEADS, L), 0) < MLSTM_HEADS
    subs = _sub_tiles(x_ref.shape[0])
    hns = [_rms(x_ref[r, :], gpre_ref[...]).astype(BF16) for r in subs]
    gates = [_dot_nt(wgt_ref[...], hn) + bgt_ref[...] for hn in hns]
    projs = [_dot_nt(wt_ref[...], hns[0])]
    cums = [_dot(_bf16_parts(_log_sigmoid(g), axis=0), tri) for g in gates]
    projs += [_dot_nt(wt_ref[...], hn) for hn in hns[1:]]
    for r, g, cum in zip(subs, gates, cums):
        b_rows = cum[:8] + cum[8:16] + cum[16:]
        gb_ref[:, r] = jnp.where(is_input_gate, g, b_rows)
        c_rows = g - pltpu.roll(b_rows, MLSTM_HEADS, axis=0)
        cc_ref[r, :] = _dot_nt(eye_ref[...], _bf16_parts(c_rows, axis=1))
    for r, hn, proj_t in zip(subs, hns, projs):
        k_ref[r, :] = _dot(hn, wk_ref[...]).astype(BF16)
        qt_ref[:, r] = (proj_t[:nq] * MLSTM_DK ** -0.5).astype(BF16)
        vt_ref[:, r] = proj_t[nq:nq + nv].astype(BF16)
        sgt_ref[:, r] = jax.nn.sigmoid(proj_t[nq + nv:])


def _mlstm_proj(x, params, tm=ROW_TILE):
    assert SUB_ROWS == MLSTM_CHUNK
    t, d = x.shape
    nq = MLSTM_HEADS * MLSTM_DK
    nv = MLSTM_HEADS * MLSTM_DV
    ng = 2 * MLSTM_HEADS
    arrays, specs = _split(params)
    eye3 = jnp.tile(jnp.eye(MLSTM_CHUNK, dtype=BF16), (1, 3))
    rows = lambda n: pl.BlockSpec((tm, n), lambda i: (i, 0))
    cols = lambda n: pl.BlockSpec((n, tm), lambda i: (0, i))
    return pl.pallas_call(
        _mlstm_proj_kernel,
        grid=(t // tm,),
        in_specs=[rows(d)] + specs + [pl.BlockSpec(eye3.shape, lambda i: (0, 0), pipeline_mode=pl.Buffered(1))],
        out_specs=[rows(nq), cols(nq), cols(nv), cols(nv), cols(ng), rows(ng)],
        out_shape=[jax.ShapeDtypeStruct((t, nq), BF16), jax.ShapeDtypeStruct((nq, t), BF16),
                   jax.ShapeDtypeStruct((nv, t), BF16), jax.ShapeDtypeStruct((nv, t), F32),
                   jax.ShapeDtypeStruct((ng, t), F32), jax.ShapeDtypeStruct((t, ng), F32)],
        compiler_params=_params("parallel"),
        name="mlstm_proj",
    )(x, *arrays, eye3)


def _mlstm_scan_kernel(k_ref, qt_ref, vt_ref, sgt_ref, gb_ref, cc_ref, hn_ref, o_ref, state_ref, m_ref):
    L = MLSTM_CHUNK
    H, DK, DV = MLSTM_HEADS, MLSTM_DK, MLSTM_DV

    @pl.when(pl.program_id(1) == 0)
    def _():
        state_ref[...] = jnp.zeros_like(state_ref)
        m_ref[...] = jnp.zeros_like(m_ref)

    gb = gb_ref[...]
    cc = cc_ref[...]
    causal_t = (lax.broadcasted_iota(jnp.int32, (L, L), 0)
                <= lax.broadcasted_iota(jnp.int32, (L, L), 1))

    ks = [k_ref[:, h * DK:(h + 1) * DK] for h in range(H)]
    qts = [qt_ref[h * DK:(h + 1) * DK, :] for h in range(H)]
    qk, inter, gates = [], [], []
    for h in range(H):
        qk.append(_dot(ks[h], qts[h]))
        inter.append(_dot(state_ref[h].astype(BF16), qts[h]))
        b_r = gb[H + h:H + h + 1, :]
        c_c = cc[:, h:h + 1]
        m = m_ref[h][:, 0:1]
        d = jnp.where(causal_t, b_r + c_c, -jnp.inf)
        g = b_r + m
        mt = jnp.maximum(g, jnp.max(d, axis=0, keepdims=True))
        gates.append((b_r, m, mt, jnp.exp(g - mt), jnp.exp(d - mt)))

    for h in range(H):
        b_r, m, mt, w_inter, w_intra = gates[h]
        li_r = gb[h:h + 1, :]
        s = qk[h] * w_intra
        vt = vt_ref[h * DV:(h + 1) * DV, :]
        num = w_inter * inter[h][:DV] + _dot(vt, s.astype(BF16))
        den = w_inter * inter[h][DV:DV + 1] + jnp.sum(s, axis=0, keepdims=True)
        r_den = 1.0 / jnp.maximum(jnp.abs(den), jnp.exp(-mt))
        ms = jnp.mean(num * num, axis=0, keepdims=True) * (r_den * r_den)
        hcn = num * (r_den * lax.rsqrt(ms + NORM_EPS)) * hn_ref[h]
        out = sgt_ref[h * DV:(h + 1) * DV, :] * hcn
        o_ref[:, h * DV:(h + 1) * DV] = out.T.astype(o_ref.dtype)

        b_last = b_r[:, L - 1:L]
        m_new = mt[:, L - 1:L]
        w_k = jnp.exp(b_last - b_r + li_r - m_new)
        decay = jnp.exp(b_last + m - m_new)
        vt_ext = jnp.concatenate([(vt.astype(F32) * w_k).astype(BF16),
                                  jnp.broadcast_to(w_k, (8, L)).astype(BF16)], axis=0)
        state_ref[h] = decay * state_ref[h] + _dot(vt_ext, ks[h])
        m_ref[h] = jnp.broadcast_to(m_new, (1, LANE))


def _mlstm_scan(k, qt, vt, sgt, gb, cc, hnorm, batch, seq):
    t = k.shape[0]
    L = MLSTM_CHUNK
    nc = seq // L
    nq = MLSTM_HEADS * MLSTM_DK
    nv = MLSTM_HEADS * MLSTM_DV
    arrays, specs = _split([hnorm])
    rows = lambda n: pl.BlockSpec((L, n), lambda b, c: (b * nc + c, 0))
    cols = lambda n: pl.BlockSpec((n, L), lambda b, c: (0, b * nc + c))
    return pl.pallas_call(
        _mlstm_scan_kernel,
        grid=(batch, nc),
        in_specs=[rows(nq), cols(nq), cols(nv), cols(nv), cols(2 * MLSTM_HEADS), rows(2 * MLSTM_HEADS)] + specs,
        out_specs=rows(nv),
        out_shape=jax.ShapeDtypeStruct((t, nv), BF16),
        scratch_shapes=[pltpu.VMEM((MLSTM_HEADS, MLSTM_DV + 8, MLSTM_DK), F32),
                        pltpu.VMEM((MLSTM_HEADS, 1, LANE), F32)],
        compiler_params=_params("parallel", "arbitrary"),
        name="mlstm_scan",
    )(k, qt, vt, sgt, gb, cc, *arrays)


def _mix_out_mem_xattn_kernel(a_ref, x_ref, kv_ref, wmix_ref, gmix_ref, gq_ref, wq_ref, wo_ref, gpost_ref,
                              o_ref, oc_ref):
    subs = _sub_tiles(x_ref.shape[0])
    nk = MEM_HEADS * MEM_DH
    heads = [(h * MEM_DH, (h + 1) * MEM_DH) for h in range(MEM_HEADS)]
    mix = [_dot(a_ref[r, :], wmix_ref[...]) for r in subs]
    xs = [x_ref[r, :] + _rms(y, gmix_ref[...]) for r, y in zip(subs, mix)]
    hn = [_rms(x, gq_ref[...]).astype(BF16) for x in xs]
    qs = [(_dot(h, wq_ref[...]) * MEM_DH ** -0.5).astype(BF16) for h in hn]
    scores = [[_dot_nt(q[:, lo:hi], kv_ref[:, lo:hi]) for lo, hi in heads] for q in qs]
    for r, x, sub_scores in zip(subs, xs, scores):
        for (lo, hi), s in zip(heads, sub_scores):
            e = jnp.exp(s - jnp.max(s, axis=-1, keepdims=True))
            p = e / jnp.sum(e, axis=-1, keepdims=True)
            oc_ref[r, lo:hi] = _dot(p.astype(BF16), kv_ref[:, nk + lo:nk + hi]).astype(BF16)
        o_ref[r, :] = x + _rms(_dot(oc_ref[r, :], wo_ref[...]), gpost_ref[...])


def _mix_out_mem_xattn(a, x, kv, params, batch, seq, tm=BIG_ROW_TILE):
    t, d = x.shape
    nt = seq // tm
    arrays, specs = _split(params)
    rows = lambda n: pl.BlockSpec((tm, n), lambda b, i: (b * nt + i, 0))
    return pl.pallas_call(
        _mix_out_mem_xattn_kernel,
        grid=(batch, nt),
        in_specs=[rows(a.shape[1]), rows(d), pl.BlockSpec((N_MEM, kv.shape[1]), lambda b, i: (b, 0))] + specs,
        out_specs=rows(d),
        out_shape=jax.ShapeDtypeStruct((t, d), F32),
        scratch_shapes=[pltpu.VMEM((tm, MEM_HEADS * MEM_DH), BF16)],
        compiler_params=_params("parallel", "parallel"),
        name="mix_out_mem_xattn",
    )(a, x, kv, *arrays)


def _swiglu_kernel(x_ref, gpre_ref, wgu_ref, wd_ref, gpost_ref, o_ref, act_ref):
    subs = _sub_tiles(x_ref.shape[0])
    hn = [_rms(x_ref[r, :], gpre_ref[...]).astype(BF16) for r in subs]
    for c in range(D_FF // FF_CHUNK):
        lo, hi = c * FF_CHUNK, (c + 1) * FF_CHUNK
        for r, h in zip(subs, hn):
            gate = _dot(h, wgu_ref[:, lo:hi])
            up = _dot(h, wgu_ref[:, D_FF + lo:D_FF + hi])
            act_ref[r, lo:hi] = (gate * jax.nn.sigmoid(gate) * up).astype(BF16)
    down = [_dot(act_ref[r, :], wd_ref[...]) for r in subs]
    for r, y in zip(subs, down):
        o_ref[r, :] = x_ref[r, :] + _rms(y, gpost_ref[...])


def _swiglu(x, params, tm=BIG_ROW_TILE):
    t, d = x.shape
    arrays, specs = _split(params)
    return pl.pallas_call(
        _swiglu_kernel,
        grid=(t // tm,),
        in_specs=[pl.BlockSpec((tm, d), lambda i: (i, 0))] + specs,
        out_specs=pl.BlockSpec((tm, d), lambda i: (i, 0)),
        out_shape=jax.ShapeDtypeStruct((t, d), F32),
        scratch_shapes=[pltpu.VMEM((tm, D_FF), BF16)],
        compiler_params=_params("parallel"),
        name="swiglu",
    )(x, *arrays)


def _pad_last(w, n):
    return jnp.pad(w, [(0, 0)] * (w.ndim - 1) + [(0, n - w.shape[-1])])


def _swap_halves(w):
    half = w.shape[-1] // 2
    return jnp.concatenate([w[..., half:], w[..., :half]], axis=-1)


def _transpose(w):
    return jnp.swapaxes(w, -1, -2)


def _mla_weights(w_in, w_uq, w_ukv):
    n = w_in.shape[0]
    r = MLA_Q_RANK + MLA_KV_RANK
    w_kr = w_in[..., r:]
    win = jnp.concatenate([w_in[..., :r], _pad_last(w_kr, LANE), _pad_last(_swap_halves(w_kr), LANE)], axis=-1)
    uq = w_uq.reshape(n, MLA_Q_RANK, MLA_HEADS, MLA_NOPE + MLA_ROPE)
    wqn = uq[..., :MLA_NOPE].reshape(n, MLA_Q_RANK, MLA_HEADS * LANE)
    rope = uq[..., MLA_NOPE:]
    wqr = rope.reshape(n, MLA_Q_RANK, MLA_HEADS * MLA_ROPE)
    wqs = _swap_halves(rope).reshape(n, MLA_Q_RANK, MLA_HEADS * MLA_ROPE)
    ukv = w_ukv.reshape(n, MLA_KV_RANK, MLA_HEADS, 2, MLA_NOPE)
    wk = ukv[:, :, :, 0, :].reshape(n, MLA_KV_RANK, MLA_HEADS * MLA_NOPE)
    wvt = _transpose(ukv[:, :, :, 1, :].reshape(n, MLA_KV_RANK, MLA_HEADS * MLA_V))
    return tuple(w.astype(BF16) for w in (win, _transpose(wqn), _transpose(wqr), _transpose(wqs), wk, wvt))


def _mlstm_weights(w_in, b_gates):
    nq = MLSTM_HEADS * MLSTM_DK
    nv = MLSTM_HEADS * MLSTM_DV
    w_q, w_k = w_in[..., :nq], w_in[..., nq:2 * nq]
    w_vo = w_in[..., 2 * nq:2 * nq + 2 * nv]
    w_g = w_in[..., 2 * nq + 2 * nv:]
    w_t = _transpose(jnp.concatenate([w_q, w_vo], axis=-1)).astype(BF16)
    return w_k.astype(BF16), w_t, _transpose(w_g).astype(BF16), b_gates[:, :, None]


def kernel(x, mem, positions, mla_w_in, mla_q_norm, mla_w_uq, mla_kv_norm, mla_w_ukv, mla_w_o, mlstm_w_in, mlstm_b_gates, mlstm_head_norm, mlstm_w_o, norm_mix_pre, norm_mix_post, norm_mem_q, norm_mem_kv, norm_mem_post, norm_ffn_pre, norm_ffn_post, mem_w_q, mem_w_kv, mem_w_o, ffn_w_gate_up, ffn_w_down):
    batch, seq, d = x.shape
    t = batch * seq
    xf = x.reshape(t, d)
    memf = mem.reshape(batch * mem.shape[1], d)
    tables = _rope_tables(positions.reshape(1, t))

    gain = lambda g: g[:, None, :]
    bf16 = lambda w: w.astype(BF16)
    g_mix_pre, g_mix_post = gain(norm_mix_pre), gain(norm_mix_post)
    g_mem_q, g_mem_kv, g_mem_post = gain(norm_mem_q), gain(norm_mem_kv), gain(norm_mem_post)
    g_ffn_pre, g_ffn_post = gain(norm_ffn_pre), gain(norm_ffn_post)
    mla_w = _mla_weights(mla_w_in, mla_w_uq, mla_w_ukv)
    mla_gq, mla_gkv, mla_wo = gain(mla_q_norm), gain(mla_kv_norm), bf16(mla_w_o)
    mlstm_w = _mlstm_weights(mlstm_w_in, mlstm_b_gates)
    mlstm_hn, mlstm_wo = mlstm_head_norm[..., None], bf16(mlstm_w_o)
    mem_wq, mem_wkv, mem_wo = bf16(mem_w_q), bf16(mem_w_kv), bf16(mem_w_o)
    ffn_wgu, ffn_wd = bf16(ffn_w_gate_up), bf16(ffn_w_down)

    for i in range(DEPTH):
        j = i // 2
        if i % 2 == 0:
            win, wqn, wqr, wqs, wk, wvt = mla_w
            qt, k, vt = _mla_proj(xf, tables, [(g_mix_pre, i), (win, j), (mla_gq, j), (wqn, j), (wqr, j),
                                               (wqs, j), (mla_gkv, j), (wk, j), (wvt, j)])
            a = _mla_attn(qt, k, vt, batch, seq)
            w_mix = mla_wo
        else:
            wk, wt, wgt, bgt = mlstm_w
            k, qt, vt, sgt, gb, cc = _mlstm_proj(xf, [(g_mix_pre, i), (wk, j), (wt, j), (wgt, j), (bgt, j)])
            a = _mlstm_scan(k, qt, vt, sgt, gb, cc, (mlstm_hn, j), batch, seq)
            w_mix = mlstm_wo

        kv = _norm_matmul(memf, [(g_mem_kv, i), (mem_wkv, i)])
        xf = _mix_out_mem_xattn(a, xf, kv, [(w_mix, j), (g_mix_post, i), (g_mem_q, i), (mem_wq, i),
                                            (mem_wo, i), (g_mem_post, i)], batch, seq)
        xf = _swiglu(xf, [(g_ffn_pre, i), (ffn_wgu, i), (ffn_wd, i), (g_ffn_post, i)])
    return xf.reshape(batch, seq, d)
```

```python
import functools

import jax
import jax.numpy as jnp
from jax import lax
from jax.experimental import pallas as pl
from jax.experimental.pallas import tpu as pltpu

F32 = jnp.float32
BF16 = jnp.bfloat16

D_MODEL = 1024
DEPTH = 4
NORM_EPS = 1e-6

MLA_HEADS = 8
MLA_NOPE = 128
MLA_ROPE = 64
MLA_V = 128
MLA_V_ROWS = MLA_V + 16
MLA_Q_RANK = 384
MLA_KV_RANK = 256
MLA_QK_PAD = 256
ROPE_BASE = 10000.0

MLSTM_HEADS = 4
MLSTM_DV = 256
MLSTM_DK = 128
MLSTM_CHUNK = 256
MLSTM_STEP_CHUNKS = 4

N_MEM = 256
MEM_HEADS = 4
MEM_DH = 256

D_FF = 2816
FF_CHUNK = 256

ROW_TILE = 512
BIG_ROW_TILE = 1024
SUB_ROWS = 256
ATTN_TILE = 512
MLA_Q_SCALE = (MLA_NOPE + MLA_ROPE) ** -0.5 * 1.4426950408889634
LANE = 128
VMEM_LIMIT = 56 * 1024 * 1024


def _rms(x, g):
    ms = jnp.mean(x * x, axis=-1, keepdims=True)
    return x * lax.rsqrt(ms + NORM_EPS) * g


def _dot(a, b):
    return jnp.dot(a, b, preferred_element_type=F32)


def _dot_nt(a, b):
    return lax.dot_general(a, b, (((1,), (1,)), ((), ())), preferred_element_type=F32)


def _layer_spec(w, layer):
    nd = w.ndim - 1
    return pl.BlockSpec((None,) + w.shape[1:], lambda *_: (layer,) + (0,) * nd, pipeline_mode=pl.Buffered(1))


def _split(params):
    return [w for w, _ in params], [_layer_spec(w, layer) for w, layer in params]


def _sub_tiles(rows):
    return [pl.ds(i * SUB_ROWS, SUB_ROWS) for i in range(rows // SUB_ROWS)]


def _params(*sem):
    return pltpu.CompilerParams(dimension_semantics=sem, vmem_limit_bytes=VMEM_LIMIT)


def _rope_table_kernel(pos_ref, freq_ref, cos_t_ref, sin_t_ref, cos_ref, sin_ref):
    ang = freq_ref[...] * pos_ref[...].astype(F32)
    c, s = jnp.cos(ang), jnp.sin(ang)
    zeros = jnp.zeros((LANE - MLA_ROPE, ang.shape[1]), F32)
    cos_t = jnp.concatenate([c, c, zeros], axis=0)
    sin_t = jnp.concatenate([-s, s, zeros], axis=0)
    cos_t_ref[...] = cos_t
    sin_t_ref[...] = sin_t
    cos_ref[...] = cos_t.T
    sin_ref[...] = sin_t.T


def _rope_tables(pos_row):
    t = pos_row.shape[1]
    half = MLA_ROPE // 2
    inv_freq = ROPE_BASE ** (-jnp.arange(0, MLA_ROPE, 2, dtype=F32) / MLA_ROPE)
    tm = 1024
    return pl.pallas_call(
        _rope_table_kernel,
        grid=(t // tm,),
        in_specs=[pl.BlockSpec((1, tm), lambda i: (0, i)), pl.BlockSpec((half, 1), lambda i: (0, 0))],
        out_specs=[pl.BlockSpec((LANE, tm), lambda i: (0, i))] * 2 + [pl.BlockSpec((tm, LANE), lambda i: (i, 0))] * 2,
        out_shape=[jax.ShapeDtypeStruct((LANE, t), F32)] * 2 + [jax.ShapeDtypeStruct((t, LANE), F32)] * 2,
        compiler_params=_params("parallel"),
        name="rope_tables",
    )(pos_row, inv_freq[:, None])


def _norm_matmul_kernel(x_ref, g_ref, w_ref, o_ref):
    hn = _rms(x_ref[...], g_ref[...]).astype(BF16)
    o_ref[...] = _dot(hn, w_ref[...]).astype(o_ref.dtype)


def _norm_matmul(x, params, tm=ROW_TILE):
    t, d = x.shape
    arrays, specs = _split(params)
    n = arrays[-1].shape[-1]
    return pl.pallas_call(
        _norm_matmul_kernel,
        grid=(t // tm,),
        in_specs=[pl.BlockSpec((tm, d), lambda i: (i, 0))] + specs,
        out_specs=pl.BlockSpec((tm, n), lambda i: (i, 0)),
        out_shape=jax.ShapeDtypeStruct((t, n), BF16),
        compiler_params=_params("parallel"),
        name="norm_matmul",
    )(x, *arrays)


def _mla_proj_kernel(x_ref, cos_t_ref, sin_t_ref, cos_ref, sin_ref, gpre_ref, win_ref, gq_ref, wqn_ref,
                     wqr_ref, wqs_ref, gkv_ref, wk_ref, wvt_ref, qt_ref, k_ref, vt_ref):
    subs = _sub_tiles(x_ref.shape[0])
    hns = [_rms(x_ref[r, :], gpre_ref[...]).astype(BF16) for r in subs]
    projs = [_dot(hn, win_ref[...]) for hn in hns]
    for r, proj in zip(subs, projs):
        cq = proj[:, :MLA_Q_RANK]
        ckv = proj[:, MLA_Q_RANK:MLA_Q_RANK + MLA_KV_RANK]
        kr = proj[:, MLA_Q_RANK + MLA_KV_RANK:MLA_Q_RANK + MLA_KV_RANK + LANE]
        krs = proj[:, MLA_Q_RANK + MLA_KV_RANK + LANE:]
        k_rope = (kr * cos_ref[r, :] + krs * sin_ref[r, :]).astype(BF16)

        cqn = _rms(cq, gq_ref[...]).astype(BF16)
        q_nope = _dot_nt(wqn_ref[...], cqn)
        q_r = _dot_nt(wqr_ref[...], cqn)
        q_s = _dot_nt(wqs_ref[...], cqn)
        ckvn = _rms(ckv, gkv_ref[...]).astype(BF16)
        k_nope = _dot(ckvn, wk_ref[...])
        v_t = _dot_nt(wvt_ref[...], ckvn).astype(BF16)
        ones = jnp.ones((MLA_V_ROWS - MLA_V, v_t.shape[1]), BF16)
        for h in range(MLA_HEADS):
            vt_ref[0, h * MLA_V_ROWS:h * MLA_V_ROWS + MLA_V, r] = v_t[h * MLA_V:(h + 1) * MLA_V]
            vt_ref[0, h * MLA_V_ROWS + MLA_V:(h + 1) * MLA_V_ROWS, r] = ones

        cos_t = cos_t_ref[:MLA_ROPE, r] * MLA_Q_SCALE
        sin_t = sin_t_ref[:MLA_ROPE, r] * MLA_Q_SCALE
        zeros = jnp.zeros((MLA_QK_PAD - MLA_NOPE - MLA_ROPE, cos_t.shape[1]), BF16)
        for h in range(MLA_HEADS):
            lo, hi = h * LANE, (h + 1) * LANE
            rlo, rhi = h * MLA_ROPE, (h + 1) * MLA_ROPE
            base = h * MLA_QK_PAD
            qt_ref[0, base:base + MLA_NOPE, r] = (q_nope[lo:hi] * MLA_Q_SCALE).astype(BF16)
            qt_ref[0, base + MLA_NOPE:base + MLA_NOPE + MLA_ROPE, r] = (
                q_r[rlo:rhi] * cos_t + q_s[rlo:rhi] * sin_t).astype(BF16)
            qt_ref[0, base + MLA_NOPE + MLA_ROPE:base + MLA_QK_PAD, r] = zeros
            k_ref[r, base:base + LANE] = k_nope[:, lo:hi].astype(BF16)
            k_ref[r, base + LANE:base + 2 * LANE] = k_rope


def _mla_proj(x, tables, params, tm=ATTN_TILE):
    t, d = x.shape
    hq = MLA_HEADS * MLA_QK_PAD
    hv = MLA_HEADS * MLA_V_ROWS
    arrays, specs = _split(params)
    rows = lambda n: pl.BlockSpec((tm, n), lambda i: (i, 0))
    cols = lambda n: pl.BlockSpec((n, tm), lambda i: (0, i))
    tiles = lambda n: pl.BlockSpec((1, n, tm), lambda i: (i, 0, 0))
    return pl.pallas_call(
        _mla_proj_kernel,
        grid=(t // tm,),
        in_specs=[rows(d), cols(LANE), cols(LANE), rows(LANE), rows(LANE)] + specs,
        out_specs=[tiles(hq), rows(hq), tiles(hv)],
        out_shape=[jax.ShapeDtypeStruct((t // tm, hq, tm), BF16), jax.ShapeDtypeStruct((t, hq), BF16),
                   jax.ShapeDtypeStruct((t // tm, hv, tm), BF16)],
        compiler_params=_params("parallel"),
        name="mla_proj",
    )(x, *tables, *arrays)


def _mla_attn_kernel(q_ref, k_ref, vt_ref, *rest, tile, n_full):
    o_ref = rest[-1]
    half = tile // 2
    chains = [(g, c) for g in range(MLA_HEADS) for c in range(2)]
    qs = [q_ref[0, g * MLA_QK_PAD:(g + 1) * MLA_QK_PAD, c * half:(c + 1) * half] for g, c in chains]

    def scores(kb, q, mask=None):
        s = _dot(kb, q)
        return s if mask is None else jnp.where(mask, s, -jnp.inf)

    def accumulate(carry, s, vtb):
        m, acc = carry
        m_new = jnp.maximum(m, jnp.max(s, axis=0, keepdims=True))
        p = jnp.exp2(s - m_new)
        acc = jnp.exp2(m - m_new) * acc + _dot(vtb, p.astype(BF16))
        return m_new, acc

    def k_block(g, start, size):
        return k_ref[start:start + size, g * MLA_QK_PAD:(g + 1) * MLA_QK_PAD]

    def causal(keys, shift):
        row = lax.broadcasted_iota(jnp.int32, (keys, half), 0)
        col = lax.broadcasted_iota(jnp.int32, (keys, half), 1)
        return row <= col + shift

    def block_scores(j, n):
        g, c = chains[n]
        if j < n_full:
            return scores(k_block(g, j * tile, tile), qs[n])
        keys = (half, tile)[c]
        return scores(k_block(g, j * tile, keys), qs[n], causal(keys, c * half))

    def block_vt(j, n):
        g, c = chains[n]
        keys = tile if j < n_full else (half, tile)[c]
        return vt_ref[j, g * MLA_V_ROWS:(g + 1) * MLA_V_ROWS, :keys]

    init = (jnp.full((1, half), -jnp.inf, F32), jnp.zeros((MLA_V_ROWS, half), F32))
    carries = [init] * len(chains)
    ss = [block_scores(0, n) for n in range(len(chains))]
    for j in range(n_full):
        for n in range(len(chains)):
            carries[n] = accumulate(carries[n], ss[n], block_vt(j, n))
            ss[n] = block_scores(j + 1, n)

    base = n_full * tile if o_ref.shape[0] > tile else 0
    if o_ref.shape[0] > tile:
        o_ref[...] = jnp.zeros_like(o_ref)
    for n, (g, c) in enumerate(chains):
        _, acc = accumulate(carries[n], ss[n], block_vt(n_full, n))
        out = acc[:MLA_V] / acc[MLA_V:MLA_V + 1]
        o_ref[base + c * half:base + (c + 1) * half, g * MLA_V:(g + 1) * MLA_V] = out.T.astype(o_ref.dtype)


def _mla_attn(qt, k, vt, batch, seq, tile=ATTN_TILE):
    t = k.shape[0]
    nq = seq // tile
    out = None
    for i in reversed(range(nq)):
        kv_tiles = i + 1 if nq % (i + 1) == 0 else nq
        per_batch = nq // kv_tiles
        in_specs = [pl.BlockSpec((1, MLA_HEADS * MLA_QK_PAD, tile), lambda b, i=i: (b * nq + i, 0, 0)),
                    pl.BlockSpec((kv_tiles * tile, MLA_HEADS * MLA_QK_PAD), lambda b, n=per_batch: (b * n, 0)),
                    pl.BlockSpec((kv_tiles, MLA_HEADS * MLA_V_ROWS, tile), lambda b, n=per_batch: (b * n, 0, 0))]
        operands = [qt, k, vt]
        if out is None:
            out_spec = pl.BlockSpec((seq, MLA_HEADS * MLA_V), lambda b: (b, 0))
        else:
            out_spec = pl.BlockSpec((tile, MLA_HEADS * MLA_V), lambda b, i=i: (b * nq + i, 0))
            in_specs.append(pl.BlockSpec(memory_space=pl.ANY))
            operands.append(out)
        out = pl.pallas_call(
            functools.partial(_mla_attn_kernel, tile=tile, n_full=i),
            grid=(batch,),
            in_specs=in_specs,
            out_specs=out_spec,
            out_shape=jax.ShapeDtypeStruct((t, MLA_HEADS * MLA_V), BF16),
            input_output_aliases={3: 0} if len(operands) == 4 else {},
            compiler_params=_params("parallel"),
            name=f"mla_attn_q{i}",
        )(*operands)
    return out


def _log_sigmoid(x):
    return jnp.minimum(x, 0.0) - jnp.log1p(jnp.exp(-jnp.abs(x)))


def _bf16_parts(x, axis):
    parts = []
    for _ in range(3):
        p = x.astype(BF16).astype(F32)
        parts.append(p)
        x = x - p
    return jnp.concatenate(parts, axis=axis).astype(BF16)


def _mlstm_proj_kernel(x_ref, gpre_ref, wk_ref, wt_ref, wgt_ref, bgt_ref, eye_ref,
                       k_ref, qt_ref, vt_ref, sgt_ref, gb_ref, cc_ref):
    nq = MLSTM_HEADS * MLSTM_DK
    nv = MLSTM_HEADS * MLSTM_DV
    L = MLSTM_CHUNK
    tri = (lax.broadcasted_iota(jnp.int32, (L, L), 0) <= lax.broadcasted_iota(jnp.int32, (L, L), 1)).astype(BF16)
    is_input_gate = lax.broadcasted_iota(jnp.int32, (2 * MLSTM_HEADS, L), 0) < MLSTM_HEADS
    subs = _sub_tiles(x_ref.shape[0])
    hns = [_rms(x_ref[r, :], gpre_ref[...]).astype(BF16) for r in subs]
    gates = [_dot_nt(wgt_ref[...], hn) + bgt_ref[...] for hn in hns]
    projs = [_dot_nt(wt_ref[...], hns[0])]
    cums = [_dot(_bf16_parts(_log_sigmoid(g), axis=0), tri) for g in gates]
    projs += [_dot_nt(wt_ref[...], hn) for hn in hns[1:]]
    for r, g, cum in zip(subs, gates, cums):
        b_rows = cum[:8] + cum[8:16] + cum[16:]
        gb_ref[:, r] = jnp.where(is_input_gate, g, b_rows)
        c_rows = g - pltpu.roll(b_rows, MLSTM_HEADS, axis=0)
        cc_ref[r, :] = _dot_nt(eye_ref[...], _bf16_parts(c_rows, axis=1))
    for r, hn, proj_t in zip(subs, hns, projs):
        k_ref[r, :] = _dot(hn, wk_ref[...]).astype(BF16)
        qt_ref[:, r] = (proj_t[:nq] * MLSTM_DK ** -0.5).astype(BF16)
        vt_ref[:, r] = proj_t[nq:nq + nv].astype(BF16)
        sgt_ref[:, r] = jax.nn.sigmoid(proj_t[nq + nv:])


def _mlstm_proj(x, params, tm=BIG_ROW_TILE):
    assert SUB_ROWS == MLSTM_CHUNK
    t, d = x.shape
    nq = MLSTM_HEADS * MLSTM_DK
    nv = MLSTM_HEADS * MLSTM_DV
    ng = 2 * MLSTM_HEADS
    arrays, specs = _split(params)
    eye3 = jnp.tile(jnp.eye(MLSTM_CHUNK, dtype=BF16), (1, 3))
    rows = lambda n: pl.BlockSpec((tm, n), lambda i: (i, 0))
    cols = lambda n: pl.BlockSpec((n, tm), lambda i: (0, i))
    return pl.pallas_call(
        _mlstm_proj_kernel,
        grid=(t // tm,),
        in_specs=[rows(d)] + specs + [pl.BlockSpec(eye3.shape, lambda i: (0, 0), pipeline_mode=pl.Buffered(1))],
        out_specs=[rows(nq), cols(nq), cols(nv), cols(nv), cols(ng), rows(ng)],
        out_shape=[jax.ShapeDtypeStruct((t, nq), BF16), jax.ShapeDtypeStruct((nq, t), BF16),
                   jax.ShapeDtypeStruct((nv, t), BF16), jax.ShapeDtypeStruct((nv, t), F32),
                   jax.ShapeDtypeStruct((ng, t), F32), jax.ShapeDtypeStruct((t, ng), F32)],
        compiler_params=_params("parallel"),
        name="mlstm_proj",
    )(x, *arrays, eye3)


def _mlstm_scan_kernel(k_ref, qt_ref, vt_ref, sgt_ref, gb_ref, cc_ref, hn_ref, o_ref, state_ref, m_ref):
    L = MLSTM_CHUNK
    H, DK, DV = MLSTM_HEADS, MLSTM_DK, MLSTM_DV

    @pl.when(pl.program_id(1) == 0)
    def _():
        state_ref[...] = jnp.zeros_like(state_ref)
        m_ref[...] = jnp.zeros_like(m_ref)

    causal_t = (lax.broadcasted_iota(jnp.int32, (L, L), 0)
                <= lax.broadcasted_iota(jnp.int32, (L, L), 1))

    for ci in range(MLSTM_STEP_CHUNKS):
        t0 = ci * L
        gb = gb_ref[:, t0:t0 + L]
        cc = cc_ref[t0:t0 + L, :]
        ks = [k_ref[t0:t0 + L, h * DK:(h + 1) * DK] for h in range(H)]
        qts = [qt_ref[h * DK:(h + 1) * DK, t0:t0 + L] for h in range(H)]
        qk, inter, gates = [], [], []
        for h in range(H):
            qk.append(_dot(ks[h], qts[h]))
            inter.append(_dot(state_ref[h].astype(BF16), qts[h]))
            b_r = gb[H + h:H + h + 1, :]
            c_c = cc[:, h:h + 1]
            m = m_ref[h][:, 0:1]
            d = jnp.where(causal_t, b_r + c_c, -jnp.inf)
            g = b_r + m
            mt = jnp.maximum(g, jnp.max(d, axis=0, keepdims=True))
            gates.append((b_r, m, mt, jnp.exp(g - mt), jnp.exp(d - mt)))

        for h in range(H):
            b_r, m, mt, w_inter, w_intra = gates[h]
            li_r = gb[h:h + 1, :]
            s = qk[h] * w_intra
            vt = vt_ref[h * DV:(h + 1) * DV, t0:t0 + L]
            num = w_inter * inter[h][:DV] + _dot(vt, s.astype(BF16))
            den = w_inter * inter[h][DV:DV + 1] + jnp.sum(s, axis=0, keepdims=True)
            r_den = 1.0 / jnp.maximum(jnp.abs(den), jnp.exp(-mt))
            ms = jnp.mean(num * num, axis=0, keepdims=True) * (r_den * r_den)
            hcn = num * (r_den * lax.rsqrt(ms + NORM_EPS)) * hn_ref[h]
            out = sgt_ref[h * DV:(h + 1) * DV, t0:t0 + L] * hcn
            o_ref[t0:t0 + L, h * DV:(h + 1) * DV] = out.T.astype(o_ref.dtype)

            b_last = b_r[:, L - 1:L]
            m_new = mt[:, L - 1:L]
            w_k = jnp.exp(b_last - b_r + li_r - m_new)
            decay = jnp.exp(b_last + m - m_new)
            vt_ext = jnp.concatenate([(vt.astype(F32) * w_k).astype(BF16),
                                      jnp.broadcast_to(w_k, (8, L)).astype(BF16)], axis=0)
            state_ref[h] = decay * state_ref[h] + _dot(vt_ext, ks[h])
            m_ref[h] = jnp.broadcast_to(m_new, (1, LANE))


def _mlstm_scan(k, qt, vt, sgt, gb, cc, hnorm, batch, seq):
    t = k.shape[0]
    rows_per_step = MLSTM_CHUNK * MLSTM_STEP_CHUNKS
    steps = seq // rows_per_step
    nq = MLSTM_HEADS * MLSTM_DK
    nv = MLSTM_HEADS * MLSTM_DV
    arrays, specs = _split([hnorm])
    rows = lambda n: pl.BlockSpec((rows_per_step, n), lambda b, c: (b * steps + c, 0))
    cols = lambda n: pl.BlockSpec((n, rows_per_step), lambda b, c: (0, b * steps + c))
    return pl.pallas_call(
        _mlstm_scan_kernel,
        grid=(batch, steps),
        in_specs=[rows(nq), cols(nq), cols(nv), cols(nv), cols(2 * MLSTM_HEADS), rows(2 * MLSTM_HEADS)] + specs,
        out_specs=rows(nv),
        out_shape=jax.ShapeDtypeStruct((t, nv), BF16),
        scratch_shapes=[pltpu.VMEM((MLSTM_HEADS, MLSTM_DV + 8, MLSTM_DK), F32),
                        pltpu.VMEM((MLSTM_HEADS, 1, LANE), F32)],
        compiler_params=_params("parallel", "arbitrary"),
        name="mlstm_scan",
    )(k, qt, vt, sgt, gb, cc, *arrays)


def _mix_out_mem_xattn_kernel(a_ref, x_ref, kv_ref, wmix_ref, gmix_ref, gq_ref, wq_ref, wo_ref, gpost_ref,
                              o_ref, oc_ref):
    subs = _sub_tiles(x_ref.shape[0])
    nk = MEM_HEADS * MEM_DH
    heads = [(h * MEM_DH, (h + 1) * MEM_DH) for h in range(MEM_HEADS)]
    mix = [_dot(a_ref[r, :], wmix_ref[...]) for r in subs]
    xs = [x_ref[r, :] + _rms(y, gmix_ref[...]) for r, y in zip(subs, mix)]
    hn = [_rms(x, gq_ref[...]).astype(BF16) for x in xs]
    qs = [(_dot(h, wq_ref[...]) * MEM_DH ** -0.5).astype(BF16) for h in hn]
    scores = [[_dot_nt(q[:, lo:hi], kv_ref[:, lo:hi]) for lo, hi in heads] for q in qs]
    for r, x, sub_scores in zip(subs, xs, scores):
        for (lo, hi), s in zip(heads, sub_scores):
            e = jnp.exp(s - jnp.max(s, axis=-1, keepdims=True))
            p = e / jnp.sum(e, axis=-1, keepdims=True)
            oc_ref[r, lo:hi] = _dot(p.astype(BF16), kv_ref[:, nk + lo:nk + hi]).astype(BF16)
        o_ref[r, :] = x + _rms(_dot(oc_ref[r, :], wo_ref[...]), gpost_ref[...])


def _mix_out_mem_xattn(a, x, kv, params, batch, seq, tm=BIG_ROW_TILE):
    t, d = x.shape
    nt = seq // tm
    arrays, specs = _split(params)
    rows = lambda n: pl.BlockSpec((tm, n), lambda b, i: (b * nt + i, 0))
    return pl.pallas_call(
        _mix_out_mem_xattn_kernel,
        grid=(batch, nt),
        in_specs=[rows(a.shape[1]), rows(d), pl.BlockSpec((N_MEM, kv.shape[1]), lambda b, i: (b, 0))] + specs,
        out_specs=rows(d),
        out_shape=jax.ShapeDtypeStruct((t, d), F32),
        scratch_shapes=[pltpu.VMEM((tm, MEM_HEADS * MEM_DH), BF16)],
        compiler_params=_params("parallel", "parallel"),
        name="mix_out_mem_xattn",
    )(a, x, kv, *arrays)


def _swiglu_kernel(x_ref, gpre_ref, wgu_ref, wd_ref, gpost_ref, o_ref, act_ref):
    subs = _sub_tiles(x_ref.shape[0])
    hn = [_rms(x_ref[r, :], gpre_ref[...]).astype(BF16) for r in subs]
    for c in range(D_FF // FF_CHUNK):
        lo, hi = c * FF_CHUNK, (c + 1) * FF_CHUNK
        for r, h in zip(subs, hn):
            gate = _dot(h, wgu_ref[:, lo:hi])
            up = _dot(h, wgu_ref[:, D_FF + lo:D_FF + hi])
            act_ref[r, lo:hi] = (gate * jax.nn.sigmoid(gate) * up).astype(BF16)
    down = [_dot(act_ref[r, :], wd_ref[...]) for r in subs]
    for r, y in zip(subs, down):
        o_ref[r, :] = x_ref[r, :] + _rms(y, gpost_ref[...])


def _swiglu(x, params, tm=BIG_ROW_TILE):
    t, d = x.shape
    arrays, specs = _split(params)
    return pl.pallas_call(
        _swiglu_kernel,
        grid=(t // tm,),
        in_specs=[pl.BlockSpec((tm, d), lambda i: (i, 0))] + specs,
        out_specs=pl.BlockSpec((tm, d), lambda i: (i, 0)),
        out_shape=jax.ShapeDtypeStruct((t, d), F32),
        scratch_shapes=[pltpu.VMEM((tm, D_FF), BF16)],
        compiler_params=_params("parallel"),
        name="swiglu",
    )(x, *arrays)


def _pad_last(w, n):
    return jnp.pad(w, [(0, 0)] * (w.ndim - 1) + [(0, n - w.shape[-1])])


def _swap_halves(w):
    half = w.shape[-1] // 2
    return jnp.concatenate([w[..., half:], w[..., :half]], axis=-1)


def _transpose(w):
    return jnp.swapaxes(w, -1, -2)


def _mla_weights(w_in, w_uq, w_ukv):
    n = w_in.shape[0]
    r = MLA_Q_RANK + MLA_KV_RANK
    w_kr = w_in[..., r:]
    win = jnp.concatenate([w_in[..., :r], _pad_last(w_kr, LANE), _pad_last(_swap_halves(w_kr), LANE)], axis=-1)
    uq = w_uq.reshape(n, MLA_Q_RANK, MLA_HEADS, MLA_NOPE + MLA_ROPE)
    wqn = uq[..., :MLA_NOPE].reshape(n, MLA_Q_RANK, MLA_HEADS * LANE)
    rope = uq[..., MLA_NOPE:]
    wqr = rope.reshape(n, MLA_Q_RANK, MLA_HEADS * MLA_ROPE)
    wqs = _swap_halves(rope).reshape(n, MLA_Q_RANK, MLA_HEADS * MLA_ROPE)
    ukv = w_ukv.reshape(n, MLA_KV_RANK, MLA_HEADS, 2, MLA_NOPE)
    wk = ukv[:, :, :, 0, :].reshape(n, MLA_KV_RANK, MLA_HEADS * MLA_NOPE)
    wvt = _transpose(ukv[:, :, :, 1, :].reshape(n, MLA_KV_RANK, MLA_HEADS * MLA_V))
    return tuple(w.astype(BF16) for w in (win, _transpose(wqn), _transpose(wqr), _transpose(wqs), wk, wvt))


def _mlstm_weights(w_in, b_gates):
    nq = MLSTM_HEADS * MLSTM_DK
    nv = MLSTM_HEADS * MLSTM_DV
    w_q, w_k = w_in[..., :nq], w_in[..., nq:2 * nq]
    w_vo = w_in[..., 2 * nq:2 * nq + 2 * nv]
    w_g = w_in[..., 2 * nq + 2 * nv:]
    w_t = _transpose(jnp.concatenate([w_q, w_vo], axis=-1)).astype(BF16)
    return w_k.astype(BF16), w_t, _transpose(w_g).astype(BF16), b_gates[:, :, None]


def kernel(x, mem, positions, mla_w_in, mla_q_norm, mla_w_uq, mla_kv_norm, mla_w_ukv, mla_w_o, mlstm_w_in, mlstm_b_gates, mlstm_head_norm, mlstm_w_o, norm_mix_pre, norm_mix_post, norm_mem_q, norm_mem_kv, norm_mem_post, norm_ffn_pre, norm_ffn_post, mem_w_q, mem_w_kv, mem_w_o, ffn_w_gate_up, ffn_w_down):
    batch, seq, d = x.shape
    t = batch * seq
    xf = x.reshape(t, d)
    memf = mem.reshape(batch * mem.shape[1], d)
    tables = _rope_tables(positions.reshape(1, t))

    gain = lambda g: g[:, None, :]
    bf16 = lambda w: w.astype(BF16)
    g_mix_pre, g_mix_post = gain(norm_mix_pre), gain(norm_mix_post)
    g_mem_q, g_mem_kv, g_mem_post = gain(norm_mem_q), gain(norm_mem_kv), gain(norm_mem_post)
    g_ffn_pre, g_ffn_post = gain(norm_ffn_pre), gain(norm_ffn_post)
    mla_w = _mla_weights(mla_w_in, mla_w_uq, mla_w_ukv)
    mla_gq, mla_gkv, mla_wo = gain(mla_q_norm), gain(mla_kv_norm), bf16(mla_w_o)
    mlstm_w = _mlstm_weights(mlstm_w_in, mlstm_b_gates)
    mlstm_hn, mlstm_wo = mlstm_head_norm[..., None], bf16(mlstm_w_o)
    mem_wq, mem_wkv, mem_wo = bf16(mem_w_q), bf16(mem_w_kv), bf16(mem_w_o)
    ffn_wgu, ffn_wd = bf16(ffn_w_gate_up), bf16(ffn_w_down)

    for i in range(DEPTH):
        j = i // 2
        if i % 2 == 0:
            win, wqn, wqr, wqs, wk, wvt = mla_w
            qt, k, vt = _mla_proj(xf, tables, [(g_mix_pre, i), (win, j), (mla_gq, j), (wqn, j), (wqr, j),
                                               (wqs, j), (mla_gkv, j), (wk, j), (wvt, j)])
            a = _mla_attn(qt, k, vt, batch, seq)
            w_mix = mla_wo
        else:
            wk, wt, wgt, bgt = mlstm_w
            k, qt, vt, sgt, gb, cc = _mlstm_proj(xf, [(g_mix_pre, i), (wk, j), (wt, j), (wgt, j), (bgt, j)])
            a = _mlstm_scan(k, qt, vt, sgt, gb, cc, (mlstm_hn, j), batch, seq)
            w_mix = mlstm_wo

        kv = _norm_matmul(memf, [(g_mem_kv, i), (mem_wkv, i)])
        xf = _mix_out_mem_xattn(a, xf, kv, [(w_mix, j), (g_mix_post, i), (g_mem_q, i), (mem_wq, i),
                                            (mem_wo, i), (g_mem_post, i)], batch, seq)
        xf = _swiglu(xf, [(g_ffn_pre, i), (ffn_wgu, i), (ffn_wd, i), (g_ffn_post, i)])
    return xf.reshape(batch, seq, d)
```

```python
import functools

import jax
import jax.numpy as jnp
from jax import lax
from jax.experimental import pallas as pl
from jax.experimental.pallas import tpu as pltpu

F32 = jnp.float32
BF16 = jnp.bfloat16

D_MODEL = 1024
DEPTH = 4
NORM_EPS = 1e-6

MLA_HEADS = 8
MLA_NOPE = 128
MLA_ROPE = 64
MLA_V = 128
MLA_V_ROWS = MLA_V + 16
MLA_Q_RANK = 384
MLA_KV_RANK = 256
MLA_QK_PAD = 256
ROPE_BASE = 10000.0

MLSTM_HEADS = 4
MLSTM_DV = 256
MLSTM_DK = 128
MLSTM_CHUNK = 256
MLSTM_STEP_CHUNKS = 4

N_MEM = 256
MEM_HEADS = 4
MEM_DH = 256

D_FF = 2816
FF_CHUNK = 256

ROW_TILE = 512
BIG_ROW_TILE = 1024
SUB_ROWS = 256
ATTN_TILE = 512
MLA_Q_SCALE = (MLA_NOPE + MLA_ROPE) ** -0.5 * 1.4426950408889634
LANE = 128
VMEM_LIMIT = 56 * 1024 * 1024


def _rms(x, g):
    ms = jnp.mean(x * x, axis=-1, keepdims=True)
    return x * lax.rsqrt(ms + NORM_EPS) * g


def _dot(a, b):
    return jnp.dot(a, b, preferred_element_type=F32)


def _dot_nt(a, b):
    return lax.dot_general(a, b, (((1,), (1,)), ((), ())), preferred_element_type=F32)


def _layer_spec(w, layer):
    nd = w.ndim - 1
    return pl.BlockSpec((None,) + w.shape[1:], lambda *_: (layer,) + (0,) * nd, pipeline_mode=pl.Buffered(1))


def _split(params):
    return [w for w, _ in params], [_layer_spec(w, layer) for w, layer in params]


def _sub_tiles(rows):
    return [pl.ds(i * SUB_ROWS, SUB_ROWS) for i in range(rows // SUB_ROWS)]


def _wavefront(subs, stages):
    states = [{} for _ in subs]
    for t in range(len(subs) + len(stages) - 1):
        for i, r in reversed(list(enumerate(subs))):
            if 0 <= t - i < len(stages):
                stages[t - i](r, states[i])


def _params(*sem):
    return pltpu.CompilerParams(dimension_semantics=sem, vmem_limit_bytes=VMEM_LIMIT)


def _rope_table_kernel(pos_ref, freq_ref, cos_t_ref, sin_t_ref, cos_ref, sin_ref):
    ang = freq_ref[...] * pos_ref[...].astype(F32)
    c, s = jnp.cos(ang), jnp.sin(ang)
    zeros = jnp.zeros((LANE - MLA_ROPE, ang.shape[1]), F32)
    cos_t = jnp.concatenate([c, c, zeros], axis=0)
    sin_t = jnp.concatenate([-s, s, zeros], axis=0)
    cos_t_ref[...] = cos_t
    sin_t_ref[...] = sin_t
    cos_ref[...] = cos_t.T
    sin_ref[...] = sin_t.T


def _rope_tables(pos_row):
    t = pos_row.shape[1]
    half = MLA_ROPE // 2
    inv_freq = ROPE_BASE ** (-jnp.arange(0, MLA_ROPE, 2, dtype=F32) / MLA_ROPE)
    tm = 1024
    return pl.pallas_call(
        _rope_table_kernel,
        grid=(t // tm,),
        in_specs=[pl.BlockSpec((1, tm), lambda i: (0, i)), pl.BlockSpec((half, 1), lambda i: (0, 0))],
        out_specs=[pl.BlockSpec((LANE, tm), lambda i: (0, i))] * 2 + [pl.BlockSpec((tm, LANE), lambda i: (i, 0))] * 2,
        out_shape=[jax.ShapeDtypeStruct((LANE, t), F32)] * 2 + [jax.ShapeDtypeStruct((t, LANE), F32)] * 2,
        compiler_params=_params("parallel"),
        name="rope_tables",
    )(pos_row, inv_freq[:, None])


def _norm_matmul_kernel(x_ref, g_ref, w_ref, o_ref):
    hn = _rms(x_ref[...], g_ref[...]).astype(BF16)
    o_ref[...] = _dot(hn, w_ref[...]).astype(o_ref.dtype)


def _norm_matmul(x, params, tm=ROW_TILE):
    t, d = x.shape
    arrays, specs = _split(params)
    n = arrays[-1].shape[-1]
    return pl.pallas_call(
        _norm_matmul_kernel,
        grid=(t // tm,),
        in_specs=[pl.BlockSpec((tm, d), lambda i: (i, 0))] + specs,
        out_specs=pl.BlockSpec((tm, n), lambda i: (i, 0)),
        out_shape=jax.ShapeDtypeStruct((t, n), BF16),
        compiler_params=_params("parallel"),
        name="norm_matmul",
    )(x, *arrays)


def _mla_proj_kernel(x_ref, cos_t_ref, sin_t_ref, cos_ref, sin_ref, gpre_ref, win_ref, gq_ref, wqn_ref,
                     wqr_ref, wqs_ref, gkv_ref, wk_ref, wvt_ref, qt_ref, k_ref, vt_ref):
    subs = _sub_tiles(x_ref.shape[0])
    hns = [_rms(x_ref[r, :], gpre_ref[...]).astype(BF16) for r in subs]
    projs = [_dot(hn, win_ref[...]) for hn in hns]
    for r, proj in zip(subs, projs):
        ti, tl = r.start // ATTN_TILE, pl.ds(r.start % ATTN_TILE, r.size)
        cq = proj[:, :MLA_Q_RANK]
        ckv = proj[:, MLA_Q_RANK:MLA_Q_RANK + MLA_KV_RANK]
        kr = proj[:, MLA_Q_RANK + MLA_KV_RANK:MLA_Q_RANK + MLA_KV_RANK + LANE]
        krs = proj[:, MLA_Q_RANK + MLA_KV_RANK + LANE:]
        k_rope = (kr * cos_ref[r, :] + krs * sin_ref[r, :]).astype(BF16)

        cqn = _rms(cq, gq_ref[...]).astype(BF16)
        q_nope = _dot_nt(wqn_ref[...], cqn)
        q_r = _dot_nt(wqr_ref[...], cqn)
        q_s = _dot_nt(wqs_ref[...], cqn)
        ckvn = _rms(ckv, gkv_ref[...]).astype(BF16)
        k_nope = _dot(ckvn, wk_ref[...])
        v_t = _dot_nt(wvt_ref[...], ckvn).astype(BF16)
        ones = jnp.ones((MLA_V_ROWS - MLA_V, v_t.shape[1]), BF16)
        for h in range(MLA_HEADS):
            vt_ref[ti, h * MLA_V_ROWS:h * MLA_V_ROWS + MLA_V, tl] = v_t[h * MLA_V:(h + 1) * MLA_V]
            vt_ref[ti, h * MLA_V_ROWS + MLA_V:(h + 1) * MLA_V_ROWS, tl] = ones

        cos_t = cos_t_ref[:MLA_ROPE, r] * MLA_Q_SCALE
        sin_t = sin_t_ref[:MLA_ROPE, r] * MLA_Q_SCALE
        zeros = jnp.zeros((MLA_QK_PAD - MLA_NOPE - MLA_ROPE, cos_t.shape[1]), BF16)
        for h in range(MLA_HEADS):
            lo, hi = h * LANE, (h + 1) * LANE
            rlo, rhi = h * MLA_ROPE, (h + 1) * MLA_ROPE
            base = h * MLA_QK_PAD
            qt_ref[ti, base:base + MLA_NOPE, tl] = (q_nope[lo:hi] * MLA_Q_SCALE).astype(BF16)
            qt_ref[ti, base + MLA_NOPE:base + MLA_NOPE + MLA_ROPE, tl] = (
                q_r[rlo:rhi] * cos_t + q_s[rlo:rhi] * sin_t).astype(BF16)
            qt_ref[ti, base + MLA_NOPE + MLA_ROPE:base + MLA_QK_PAD, tl] = zeros
            k_ref[r, base:base + LANE] = k_nope[:, lo:hi].astype(BF16)
            k_ref[r, base + LANE:base + 2 * LANE] = k_rope


def _mla_proj(x, tables, params, tm=BIG_ROW_TILE):
    t, d = x.shape
    hq = MLA_HEADS * MLA_QK_PAD
    hv = MLA_HEADS * MLA_V_ROWS
    arrays, specs = _split(params)
    rows = lambda n: pl.BlockSpec((tm, n), lambda i: (i, 0))
    cols = lambda n: pl.BlockSpec((n, tm), lambda i: (0, i))
    tiles = lambda n: pl.BlockSpec((tm // ATTN_TILE, n, ATTN_TILE), lambda i: (i, 0, 0))
    return pl.pallas_call(
        _mla_proj_kernel,
        grid=(t // tm,),
        in_specs=[rows(d), cols(LANE), cols(LANE), rows(LANE), rows(LANE)] + specs,
        out_specs=[tiles(hq), rows(hq), tiles(hv)],
        out_shape=[jax.ShapeDtypeStruct((t // ATTN_TILE, hq, ATTN_TILE), BF16), jax.ShapeDtypeStruct((t, hq), BF16),
                   jax.ShapeDtypeStruct((t // ATTN_TILE, hv, ATTN_TILE), BF16)],
        compiler_params=_params("parallel"),
        name="mla_proj",
    )(x, *tables, *arrays)


def _mla_attn_kernel(q_ref, k_ref, vt_ref, *rest, tile, n_full):
    o_ref = rest[-1]
    half = tile // 2
    chains = [(g, c) for g in range(MLA_HEADS) for c in range(2)]
    qs = [q_ref[0, g * MLA_QK_PAD:(g + 1) * MLA_QK_PAD, c * half:(c + 1) * half] for g, c in chains]

    def scores(kb, q, mask=None):
        s = _dot(kb, q)
        return s if mask is None else jnp.where(mask, s, -jnp.inf)

    def accumulate(carry, s, vtb):
        m, acc = carry
        m_new = jnp.maximum(m, jnp.max(s, axis=0, keepdims=True))
        p = jnp.exp2(s - m_new)
        acc = jnp.exp2(m - m_new) * acc + _dot(vtb, p.astype(BF16))
        return m_new, acc

    def k_block(g, start, size):
        return k_ref[start:start + size, g * MLA_QK_PAD:(g + 1) * MLA_QK_PAD]

    def causal(keys, shift):
        row = lax.broadcasted_iota(jnp.int32, (keys, half), 0)
        col = lax.broadcasted_iota(jnp.int32, (keys, half), 1)
        return row <= col + shift

    def block_scores(j, n):
        g, c = chains[n]
        if j < n_full:
            return scores(k_block(g, j * tile, tile), qs[n])
        keys = (half, tile)[c]
        return scores(k_block(g, j * tile, keys), qs[n], causal(keys, c * half))

    def block_vt(j, n):
        g, c = chains[n]
        keys = tile if j < n_full else (half, tile)[c]
        return vt_ref[j, g * MLA_V_ROWS:(g + 1) * MLA_V_ROWS, :keys]

    init = (jnp.full((1, half), -jnp.inf, F32), jnp.zeros((MLA_V_ROWS, half), F32))
    carries = [init] * len(chains)
    ss = [block_scores(0, n) for n in range(len(chains))]
    for j in range(n_full):
        for n in range(len(chains)):
            carries[n] = accumulate(carries[n], ss[n], block_vt(j, n))
            ss[n] = block_scores(j + 1, n)

    base = n_full * tile if o_ref.shape[0] > tile else 0
    if o_ref.shape[0] > tile:
        o_ref[...] = jnp.zeros_like(o_ref)
    for n, (g, c) in enumerate(chains):
        _, acc = accumulate(carries[n], ss[n], block_vt(n_full, n))
        out = acc[:MLA_V] / acc[MLA_V:MLA_V + 1]
        o_ref[base + c * half:base + (c + 1) * half, g * MLA_V:(g + 1) * MLA_V] = out.T.astype(o_ref.dtype)


def _mla_attn(qt, k, vt, batch, seq, tile=ATTN_TILE):
    t = k.shape[0]
    nq = seq // tile
    out = None
    for i in reversed(range(nq)):
        kv_tiles = i + 1 if nq % (i + 1) == 0 else nq
        per_batch = nq // kv_tiles
        in_specs = [pl.BlockSpec((1, MLA_HEADS * MLA_QK_PAD, tile), lambda b, i=i: (b * nq + i, 0, 0)),
                    pl.BlockSpec((kv_tiles * tile, MLA_HEADS * MLA_QK_PAD), lambda b, n=per_batch: (b * n, 0)),
                    pl.BlockSpec((kv_tiles, MLA_HEADS * MLA_V_ROWS, tile), lambda b, n=per_batch: (b * n, 0, 0))]
        operands = [qt, k, vt]
        if out is None:
            out_spec = pl.BlockSpec((seq, MLA_HEADS * MLA_V), lambda b: (b, 0))
        else:
            out_spec = pl.BlockSpec((tile, MLA_HEADS * MLA_V), lambda b, i=i: (b * nq + i, 0))
            in_specs.append(pl.BlockSpec(memory_space=pl.ANY))
            operands.append(out)
        out = pl.pallas_call(
            functools.partial(_mla_attn_kernel, tile=tile, n_full=i),
            grid=(batch,),
            in_specs=in_specs,
            out_specs=out_spec,
            out_shape=jax.ShapeDtypeStruct((t, MLA_HEADS * MLA_V), BF16),
            input_output_aliases={3: 0} if len(operands) == 4 else {},
            compiler_params=_params("parallel"),
            name=f"mla_attn_q{i}",
        )(*operands)
    return out


def _log_sigmoid(x):
    return jnp.minimum(x, 0.0) - jnp.log1p(jnp.exp(-jnp.abs(x)))


def _bf16_parts(x, axis):
    parts = []
    for _ in range(3):
        p = x.astype(BF16).astype(F32)
        parts.append(p)
        x = x - p
    return jnp.concatenate(parts, axis=axis).astype(BF16)


def _mlstm_proj_kernel(x_ref, gpre_ref, wk_ref, wt_ref, wgt_ref, bgt_ref, eye_ref,
                       k_ref, qt_ref, vt_ref, sgt_ref, gb_ref, cc_ref):
    nq = MLSTM_HEADS * MLSTM_DK
    nv = MLSTM_HEADS * MLSTM_DV
    L = MLSTM_CHUNK
    tri = (lax.broadcasted_iota(jnp.int32, (L, L), 0) <= lax.broadcasted_iota(jnp.int32, (L, L), 1)).astype(BF16)
    is_input_gate = lax.broadcasted_iota(jnp.int32, (2 * MLSTM_HEADS, L), 0) < MLSTM_HEADS
    subs = _sub_tiles(x_ref.shape[0])
    hns = [_rms(x_ref[r, :], gpre_ref[...]).astype(BF16) for r in subs]
    gates = [_dot_nt(wgt_ref[...], hn) + bgt_ref[...] for hn in hns]
    projs = [_dot_nt(wt_ref[...], hns[0])]
    cums = [_dot(_bf16_parts(_log_sigmoid(g), axis=0), tri) for g in gates]
    projs += [_dot_nt(wt_ref[...], hn) for hn in hns[1:]]
    for r, g, cum in zip(subs, gates, cums):
        b_rows = cum[:8] + cum[8:16] + cum[16:]
        gb_ref[:, r] = jnp.where(is_input_gate, g, b_rows)
        c_rows = g - pltpu.roll(b_rows, MLSTM_HEADS, axis=0)
        cc_ref[r, :] = _dot_nt(eye_ref[...], _bf16_parts(c_rows, axis=1))
    for r, hn, proj_t in zip(subs, hns, projs):
        k_ref[r, :] = _dot(hn, wk_ref[...]).astype(BF16)
        qt_ref[:, r] = (proj_t[:nq] * MLSTM_DK ** -0.5).astype(BF16)
        vt_ref[:, r] = proj_t[nq:nq + nv].astype(BF16)
        sgt_ref[:, r] = jax.nn.sigmoid(proj_t[nq + nv:])


def _mlstm_proj(x, params, tm=BIG_ROW_TILE):
    assert SUB_ROWS == MLSTM_CHUNK
    t, d = x.shape
    nq = MLSTM_HEADS * MLSTM_DK
    nv = MLSTM_HEADS * MLSTM_DV
    ng = 2 * MLSTM_HEADS
    arrays, specs = _split(params)
    eye3 = jnp.tile(jnp.eye(MLSTM_CHUNK, dtype=BF16), (1, 3))
    rows = lambda n: pl.BlockSpec((tm, n), lambda i: (i, 0))
    cols = lambda n: pl.BlockSpec((n, tm), lambda i: (0, i))
    return pl.pallas_call(
        _mlstm_proj_kernel,
        grid=(t // tm,),
        in_specs=[rows(d)] + specs + [pl.BlockSpec(eye3.shape, lambda i: (0, 0), pipeline_mode=pl.Buffered(1))],
        out_specs=[rows(nq), cols(nq), cols(nv), cols(nv), cols(ng), rows(ng)],
        out_shape=[jax.ShapeDtypeStruct((t, nq), BF16), jax.ShapeDtypeStruct((nq, t), BF16),
                   jax.ShapeDtypeStruct((nv, t), BF16), jax.ShapeDtypeStruct((nv, t), F32),
                   jax.ShapeDtypeStruct((ng, t), F32), jax.ShapeDtypeStruct((t, ng), F32)],
        compiler_params=_params("parallel"),
        name="mlstm_proj",
    )(x, *arrays, eye3)


def _mlstm_scan_kernel(k_ref, qt_ref, vt_ref, sgt_ref, gb_ref, cc_ref, hn_ref, o_ref, state_ref, m_ref):
    L = MLSTM_CHUNK
    H, DK, DV = MLSTM_HEADS, MLSTM_DK, MLSTM_DV

    @pl.when(pl.program_id(1) == 0)
    def _():
        state_ref[...] = jnp.zeros_like(state_ref)
        m_ref[...] = jnp.zeros_like(m_ref)

    causal_t = (lax.broadcasted_iota(jnp.int32, (L, L), 0)
                <= lax.broadcasted_iota(jnp.int32, (L, L), 1))

    for ci in range(MLSTM_STEP_CHUNKS):
        t0 = ci * L
        gb = gb_ref[:, t0:t0 + L]
        cc = cc_ref[t0:t0 + L, :]
        ks = [k_ref[t0:t0 + L, h * DK:(h + 1) * DK] for h in range(H)]
        qts = [qt_ref[h * DK:(h + 1) * DK, t0:t0 + L] for h in range(H)]
        qk, inter, gates = [], [], []
        for h in range(H):
            qk.append(_dot(ks[h], qts[h]))
            inter.append(_dot(state_ref[h].astype(BF16), qts[h]))
            b_r = gb[H + h:H + h + 1, :]
            c_c = cc[:, h:h + 1]
            m = m_ref[h][:, 0:1]
            d = jnp.where(causal_t, b_r + c_c, -jnp.inf)
            g = b_r + m
            mt = jnp.maximum(g, jnp.max(d, axis=0, keepdims=True))
            gates.append((b_r, m, mt, jnp.exp(g - mt), jnp.exp(d - mt)))

        for h in range(H):
            b_r, m, mt, w_inter, w_intra = gates[h]
            li_r = gb[h:h + 1, :]
            s = qk[h] * w_intra
            vt = vt_ref[h * DV:(h + 1) * DV, t0:t0 + L]
            num = w_inter * inter[h][:DV] + _dot(vt, s.astype(BF16))
            den = w_inter * inter[h][DV:DV + 1] + jnp.sum(s, axis=0, keepdims=True)
            r_den = 1.0 / jnp.maximum(jnp.abs(den), jnp.exp(-mt))
            ms = jnp.mean(num * num, axis=0, keepdims=True) * (r_den * r_den)
            hcn = num * (r_den * lax.rsqrt(ms + NORM_EPS)) * hn_ref[h]
            out = sgt_ref[h * DV:(h + 1) * DV, t0:t0 + L] * hcn
            o_ref[t0:t0 + L, h * DV:(h + 1) * DV] = out.T.astype(o_ref.dtype)

            b_last = b_r[:, L - 1:L]
            m_new = mt[:, L - 1:L]
            w_k = jnp.exp(b_last - b_r + li_r - m_new)
            decay = jnp.exp(b_last + m - m_new)
            vt_ext = jnp.concatenate([(vt.astype(F32) * w_k).astype(BF16),
                                      jnp.broadcast_to(w_k, (8, L)).astype(BF16)], axis=0)
            state_ref[h] = decay * state_ref[h] + _dot(vt_ext, ks[h])
            m_ref[h] = jnp.broadcast_to(m_new, (1, LANE))


def _mlstm_scan(k, qt, vt, sgt, gb, cc, hnorm, batch, seq):
    t = k.shape[0]
    rows_per_step = MLSTM_CHUNK * MLSTM_STEP_CHUNKS
    steps = seq // rows_per_step
    nq = MLSTM_HEADS * MLSTM_DK
    nv = MLSTM_HEADS * MLSTM_DV
    arrays, specs = _split([hnorm])
    rows = lambda n: pl.BlockSpec((rows_per_step, n), lambda b, c: (b * steps + c, 0))
    cols = lambda n: pl.BlockSpec((n, rows_per_step), lambda b, c: (0, b * steps + c))
    return pl.pallas_call(
        _mlstm_scan_kernel,
        grid=(batch, steps),
        in_specs=[rows(nq), cols(nq), cols(nv), cols(nv), cols(2 * MLSTM_HEADS), rows(2 * MLSTM_HEADS)] + specs,
        out_specs=rows(nv),
        out_shape=jax.ShapeDtypeStruct((t, nv), BF16),
        scratch_shapes=[pltpu.VMEM((MLSTM_HEADS, MLSTM_DV + 8, MLSTM_DK), F32),
                        pltpu.VMEM((MLSTM_HEADS, 1, LANE), F32)],
        compiler_params=_params("parallel", "arbitrary"),
        name="mlstm_scan",
    )(k, qt, vt, sgt, gb, cc, *arrays)


def _mix_out_mem_xattn_kernel(a_ref, x_ref, kv_ref, wmix_ref, gmix_ref, gq_ref, wq_ref, wo_ref, gpost_ref,
                              o_ref, oc_ref):
    subs = _sub_tiles(x_ref.shape[0])
    nk = MEM_HEADS * MEM_DH
    heads = [(h * MEM_DH, (h + 1) * MEM_DH) for h in range(MEM_HEADS)]

    def mix_stage(r, st):
        st["x"] = x_ref[r, :] + _rms(_dot(a_ref[r, :], wmix_ref[...]), gmix_ref[...])

    def q_stage(r, st):
        hn = _rms(st["x"], gq_ref[...]).astype(BF16)
        st["q"] = (_dot(hn, wq_ref[...]) * MEM_DH ** -0.5).astype(BF16)

    def score_stage(r, st):
        st["s"] = [_dot_nt(st["q"][:, lo:hi], kv_ref[:, lo:hi]) for lo, hi in heads]

    def pv_stage(r, st):
        for (lo, hi), s in zip(heads, st["s"]):
            e = jnp.exp(s - jnp.max(s, axis=-1, keepdims=True))
            p = e / jnp.sum(e, axis=-1, keepdims=True)
            oc_ref[r, lo:hi] = _dot(p.astype(BF16), kv_ref[:, nk + lo:nk + hi]).astype(BF16)

    def out_stage(r, st):
        o_ref[r, :] = st["x"] + _rms(_dot(oc_ref[r, :], wo_ref[...]), gpost_ref[...])

    _wavefront(subs, [mix_stage, q_stage, score_stage, pv_stage, out_stage])


def _mix_out_mem_xattn(a, x, kv, params, batch, seq, tm=BIG_ROW_TILE):
    t, d = x.shape
    nt = seq // tm
    arrays, specs = _split(params)
    rows = lambda n: pl.BlockSpec((tm, n), lambda b, i: (b * nt + i, 0))
    return pl.pallas_call(
        _mix_out_mem_xattn_kernel,
        grid=(batch, nt),
        in_specs=[rows(a.shape[1]), rows(d), pl.BlockSpec((N_MEM, kv.shape[1]), lambda b, i: (b, 0))] + specs,
        out_specs=rows(d),
        out_shape=jax.ShapeDtypeStruct((t, d), F32),
        scratch_shapes=[pltpu.VMEM((tm, MEM_HEADS * MEM_DH), BF16)],
        compiler_params=_params("parallel", "parallel"),
        name="mix_out_mem_xattn",
    )(a, x, kv, *arrays)


def _swiglu_kernel(x_ref, gpre_ref, wgu_ref, wd_ref, gpost_ref, o_ref, act_ref):
    subs = _sub_tiles(x_ref.shape[0])
    hn = [_rms(x_ref[r, :], gpre_ref[...]).astype(BF16) for r in subs]
    for c in range(D_FF // FF_CHUNK):
        lo, hi = c * FF_CHUNK, (c + 1) * FF_CHUNK
        for r, h in zip(subs, hn):
            gate = _dot(h, wgu_ref[:, lo:hi])
            up = _dot(h, wgu_ref[:, D_FF + lo:D_FF + hi])
            act_ref[r, lo:hi] = (gate * jax.nn.sigmoid(gate) * up).astype(BF16)
    down = [_dot(act_ref[r, :], wd_ref[...]) for r in subs]
    for r, y in zip(subs, down):
        o_ref[r, :] = x_ref[r, :] + _rms(y, gpost_ref[...])


def _swiglu(x, params, tm=BIG_ROW_TILE):
    t, d = x.shape
    arrays, specs = _split(params)
    return pl.pallas_call(
        _swiglu_kernel,
        grid=(t // tm,),
        in_specs=[pl.BlockSpec((tm, d), lambda i: (i, 0))] + specs,
        out_specs=pl.BlockSpec((tm, d), lambda i: (i, 0)),
        out_shape=jax.ShapeDtypeStruct((t, d), F32),
        scratch_shapes=[pltpu.VMEM((tm, D_FF), BF16)],
        compiler_params=_params("parallel"),
        name="swiglu",
    )(x, *arrays)


def _pad_last(w, n):
    return jnp.pad(w, [(0, 0)] * (w.ndim - 1) + [(0, n - w.shape[-1])])


def _swap_halves(w):
    half = w.shape[-1] // 2
    return jnp.concatenate([w[..., half:], w[..., :half]], axis=-1)


def _transpose(w):
    return jnp.swapaxes(w, -1, -2)


def _mla_weights(w_in, w_uq, w_ukv):
    n = w_in.shape[0]
    r = MLA_Q_RANK + MLA_KV_RANK
    w_kr = w_in[..., r:]
    win = jnp.concatenate([w_in[..., :r], _pad_last(w_kr, LANE), _pad_last(_swap_halves(w_kr), LANE)], axis=-1)
    uq = w_uq.reshape(n, MLA_Q_RANK, MLA_HEADS, MLA_NOPE + MLA_ROPE)
    wqn = uq[..., :MLA_NOPE].reshape(n, MLA_Q_RANK, MLA_HEADS * LANE)
    rope = uq[..., MLA_NOPE:]
    wqr = rope.reshape(n, MLA_Q_RANK, MLA_HEADS * MLA_ROPE)
    wqs = _swap_halves(rope).reshape(n, MLA_Q_RANK, MLA_HEADS * MLA_ROPE)
    ukv = w_ukv.reshape(n, MLA_KV_RANK, MLA_HEADS, 2, MLA_NOPE)
    wk = ukv[:, :, :, 0, :].reshape(n, MLA_KV_RANK, MLA_HEADS * MLA_NOPE)
    wvt = _transpose(ukv[:, :, :, 1, :].reshape(n, MLA_KV_RANK, MLA_HEADS * MLA_V))
    return tuple(w.astype(BF16) for w in (win, _transpose(wqn), _transpose(wqr), _transpose(wqs), wk, wvt))


def _mlstm_weights(w_in, b_gates):
    nq = MLSTM_HEADS * MLSTM_DK
    nv = MLSTM_HEADS * MLSTM_DV
    w_q, w_k = w_in[..., :nq], w_in[..., nq:2 * nq]
    w_vo = w_in[..., 2 * nq:2 * nq + 2 * nv]
    w_g = w_in[..., 2 * nq + 2 * nv:]
    w_t = _transpose(jnp.concatenate([w_q, w_vo], axis=-1)).astype(BF16)
    return w_k.astype(BF16), w_t, _transpose(w_g).astype(BF16), b_gates[:, :, None]


def kernel(x, mem, positions, mla_w_in, mla_q_norm, mla_w_uq, mla_kv_norm, mla_w_ukv, mla_w_o, mlstm_w_in, mlstm_b_gates, mlstm_head_norm, mlstm_w_o, norm_mix_pre, norm_mix_post, norm_mem_q, norm_mem_kv, norm_mem_post, norm_ffn_pre, norm_ffn_post, mem_w_q, mem_w_kv, mem_w_o, ffn_w_gate_up, ffn_w_down):
    batch, seq, d = x.shape
    t = batch * seq
    xf = x.reshape(t, d)
    memf = mem.reshape(batch * mem.shape[1], d)
    tables = _rope_tables(positions.reshape(1, t))

    gain = lambda g: g[:, None, :]
    bf16 = lambda w: w.astype(BF16)
    g_mix_pre, g_mix_post = gain(norm_mix_pre), gain(norm_mix_post)
    g_mem_q, g_mem_kv, g_mem_post = gain(norm_mem_q), gain(norm_mem_kv), gain(norm_mem_post)
    g_ffn_pre, g_ffn_post = gain(norm_ffn_pre), gain(norm_ffn_post)
    mla_w = _mla_weights(mla_w_in, mla_w_uq, mla_w_ukv)
    mla_gq, mla_gkv, mla_wo = gain(mla_q_norm), gain(mla_kv_norm), bf16(mla_w_o)
    mlstm_w = _mlstm_weights(mlstm_w_in, mlstm_b_gates)
    mlstm_hn, mlstm_wo = mlstm_head_norm[..., None], bf16(mlstm_w_o)
    mem_wq, mem_wkv, mem_wo = bf16(mem_w_q), bf16(mem_w_kv), bf16(mem_w_o)
    ffn_wgu, ffn_wd = bf16(ffn_w_gate_up), bf16(ffn_w_down)

    for i in range(DEPTH):
        j = i // 2
        if i % 2 == 0:
            win, wqn, wqr, wqs, wk, wvt = mla_w
            qt, k, vt = _mla_proj(xf, tables, [(g_mix_pre, i), (win, j), (mla_gq, j), (wqn, j), (wqr, j),
                                               (wqs, j), (mla_gkv, j), (wk, j), (wvt, j)])
            a = _mla_attn(qt, k, vt, batch, seq)
            w_mix = mla_wo
        else:
            wk, wt, wgt, bgt = mlstm_w
            k, qt, vt, sgt, gb, cc = _mlstm_proj(xf, [(g_mix_pre, i), (wk, j), (wt, j), (wgt, j), (bgt, j)])
            a = _mlstm_scan(k, qt, vt, sgt, gb, cc, (mlstm_hn, j), batch, seq)
            w_mix = mlstm_wo

        kv = _norm_matmul(memf, [(g_mem_kv, i), (mem_wkv, i)])
        xf = _mix_out_mem_xattn(a, xf, kv, [(w_mix, j), (g_mix_post, i), (g_mem_q, i), (mem_wq, i),
                                            (mem_wo, i), (g_mem_post, i)], batch, seq)
        xf = _swiglu(xf, [(g_ffn_pre, i), (ffn_wgu, i), (ffn_wd, i), (g_ffn_post, i)])
    return xf.reshape(batch, seq, d)
```

```python
import functools

import jax
import jax.numpy as jnp
from jax import lax
from jax.experimental import pallas as pl
from jax.experimental.pallas import tpu as pltpu

F32 = jnp.float32
BF16 = jnp.bfloat16

DEPTH = 4
NORM_EPS = 1e-6

MLA_HEADS = 8
MLA_NOPE = 128
MLA_ROPE = 64
MLA_V = 128
MLA_V_ROWS = MLA_V + 16
MLA_Q_RANK = 384
MLA_KV_RANK = 256
MLA_QK_PAD = 256
ROPE_BASE = 10000.0

MLSTM_HEADS = 4
MLSTM_DV = 256
MLSTM_DK = 128
MLSTM_CHUNK = 256
MLSTM_STEP_CHUNKS = 4

N_MEM = 256
MEM_HEADS = 4
MEM_DH = 256

D_FF = 2816
FF_CHUNK = 256

ROW_TILE = 512
BIG_ROW_TILE = 1024
SUB_ROWS = 256
ATTN_TILE = 512
MLA_Q_SCALE = (MLA_NOPE + MLA_ROPE) ** -0.5 * 1.4426950408889634
LANE = 128
VMEM_LIMIT = 56 * 1024 * 1024


def _rms(x, g):
    ms = jnp.mean(x * x, axis=-1, keepdims=True)
    return x * lax.rsqrt(ms + NORM_EPS) * g


def _dot(a, b):
    return jnp.dot(a, b, preferred_element_type=F32)


def _dot_nt(a, b):
    return lax.dot_general(a, b, (((1,), (1,)), ((), ())), preferred_element_type=F32)


def _layer_spec(w, layer):
    nd = w.ndim - 1
    return pl.BlockSpec((None,) + w.shape[1:], lambda *_: (layer,) + (0,) * nd, pipeline_mode=pl.Buffered(1))


def _split(params):
    return [w for w, _ in params], [_layer_spec(w, layer) for w, layer in params]


def _sub_tiles(rows):
    return [pl.ds(i * SUB_ROWS, SUB_ROWS) for i in range(rows // SUB_ROWS)]


def _wavefront(subs, stages):
    states = [{} for _ in subs]
    for t in range(len(subs) + len(stages) - 1):
        for i, r in reversed(list(enumerate(subs))):
            if 0 <= t - i < len(stages):
                stages[t - i](r, states[i])


def _params(*sem):
    return pltpu.CompilerParams(dimension_semantics=sem, vmem_limit_bytes=VMEM_LIMIT)


def _rope_table_kernel(pos_ref, freq_ref, cos_t_ref, sin_t_ref, cos_ref, sin_ref):
    ang = freq_ref[...] * pos_ref[...].astype(F32)
    c, s = jnp.cos(ang), jnp.sin(ang)
    zeros = jnp.zeros((LANE - MLA_ROPE, ang.shape[1]), F32)
    cos_t = jnp.concatenate([c, c, zeros], axis=0)
    sin_t = jnp.concatenate([-s, s, zeros], axis=0)
    cos_t_ref[...] = cos_t
    sin_t_ref[...] = sin_t
    cos_ref[...] = cos_t.T
    sin_ref[...] = sin_t.T


def _rope_tables(pos_row):
    t = pos_row.shape[1]
    half = MLA_ROPE // 2
    inv_freq = ROPE_BASE ** (-jnp.arange(0, MLA_ROPE, 2, dtype=F32) / MLA_ROPE)
    tm = 1024
    return pl.pallas_call(
        _rope_table_kernel,
        grid=(t // tm,),
        in_specs=[pl.BlockSpec((1, tm), lambda i: (0, i)), pl.BlockSpec((half, 1), lambda i: (0, 0))],
        out_specs=[pl.BlockSpec((LANE, tm), lambda i: (0, i))] * 2 + [pl.BlockSpec((tm, LANE), lambda i: (i, 0))] * 2,
        out_shape=[jax.ShapeDtypeStruct((LANE, t), F32)] * 2 + [jax.ShapeDtypeStruct((t, LANE), F32)] * 2,
        compiler_params=_params("parallel"),
        name="rope_tables",
    )(pos_row, inv_freq[:, None])


def _mem_kv_kernel(x_ref, g_ref, w_ref, o_ref):
    hn = _rms(x_ref[...], g_ref[...]).astype(BF16)
    o_ref[...] = _dot(hn, w_ref[...]).astype(o_ref.dtype)


def _mem_kv(x, g, w, tm=ROW_TILE):
    t, d = x.shape
    layers, _, n = w.shape
    return pl.pallas_call(
        _mem_kv_kernel,
        grid=(layers, t // tm),
        in_specs=[pl.BlockSpec((tm, d), lambda l, i: (i, 0)),
                  pl.BlockSpec((None, 1, d), lambda l, i: (l, 0, 0)),
                  pl.BlockSpec((None, d, n), lambda l, i: (l, 0, 0))],
        out_specs=pl.BlockSpec((None, tm, n), lambda l, i: (l, i, 0)),
        out_shape=jax.ShapeDtypeStruct((layers, t, n), BF16),
        compiler_params=_params("parallel", "parallel"),
        name="mem_kv",
    )(x, g, w)


def _mla_proj_kernel(x_ref, cos_t_ref, sin_t_ref, cos_ref, sin_ref, gpre_ref, win_ref, gq_ref, wqn_ref,
                     wqr_ref, wqs_ref, gkv_ref, wk_ref, wvt_ref, qt_ref, k_ref, vt_ref):
    subs = _sub_tiles(x_ref.shape[0])
    hns = [_rms(x_ref[r, :], gpre_ref[...]).astype(BF16) for r in subs]
    projs = [_dot(hn, win_ref[...]) for hn in hns]
    for r, proj in zip(subs, projs):
        ti, tl = r.start // ATTN_TILE, pl.ds(r.start % ATTN_TILE, r.size)
        cq = proj[:, :MLA_Q_RANK]
        ckv = proj[:, MLA_Q_RANK:MLA_Q_RANK + MLA_KV_RANK]
        kr = proj[:, MLA_Q_RANK + MLA_KV_RANK:MLA_Q_RANK + MLA_KV_RANK + LANE]
        krs = proj[:, MLA_Q_RANK + MLA_KV_RANK + LANE:]
        k_rope = (kr * cos_ref[r, :] + krs * sin_ref[r, :]).astype(BF16)

        cqn = _rms(cq, gq_ref[...]).astype(BF16)
        q_nope = _dot_nt(wqn_ref[...], cqn)
        q_r = _dot_nt(wqr_ref[...], cqn)
        q_s = _dot_nt(wqs_ref[...], cqn)
        ckvn = _rms(ckv, gkv_ref[...]).astype(BF16)
        k_nope = _dot(ckvn, wk_ref[...])
        v_t = _dot_nt(wvt_ref[...], ckvn).astype(BF16)
        ones = jnp.ones((MLA_V_ROWS - MLA_V, v_t.shape[1]), BF16)
        for h in range(MLA_HEADS):
            vt_ref[ti, h * MLA_V_ROWS:h * MLA_V_ROWS + MLA_V, tl] = v_t[h * MLA_V:(h + 1) * MLA_V]
            vt_ref[ti, h * MLA_V_ROWS + MLA_V:(h + 1) * MLA_V_ROWS, tl] = ones

        cos_t = cos_t_ref[:MLA_ROPE, r] * MLA_Q_SCALE
        sin_t = sin_t_ref[:MLA_ROPE, r] * MLA_Q_SCALE
        zeros = jnp.zeros((MLA_QK_PAD - MLA_NOPE - MLA_ROPE, cos_t.shape[1]), BF16)
        for h in range(MLA_HEADS):
            lo, hi = h * LANE, (h + 1) * LANE
            rlo, rhi = h * MLA_ROPE, (h + 1) * MLA_ROPE
            base = h * MLA_QK_PAD
            qt_ref[ti, base:base + MLA_NOPE, tl] = (q_nope[lo:hi] * MLA_Q_SCALE).astype(BF16)
            qt_ref[ti, base + MLA_NOPE:base + MLA_NOPE + MLA_ROPE, tl] = (
                q_r[rlo:rhi] * cos_t + q_s[rlo:rhi] * sin_t).astype(BF16)
            qt_ref[ti, base + MLA_NOPE + MLA_ROPE:base + MLA_QK_PAD, tl] = zeros
            k_ref[r, base:base + LANE] = k_nope[:, lo:hi].astype(BF16)
            k_ref[r, base + LANE:base + 2 * LANE] = k_rope


def _mla_proj(x, tables, params, tm=BIG_ROW_TILE):
    t, d = x.shape
    hq = MLA_HEADS * MLA_QK_PAD
    hv = MLA_HEADS * MLA_V_ROWS
    arrays, specs = _split(params)
    rows = lambda n: pl.BlockSpec((tm, n), lambda i: (i, 0))
    cols = lambda n: pl.BlockSpec((n, tm), lambda i: (0, i))
    tiles = lambda n: pl.BlockSpec((tm // ATTN_TILE, n, ATTN_TILE), lambda i: (i, 0, 0))
    return pl.pallas_call(
        _mla_proj_kernel,
        grid=(t // tm,),
        in_specs=[rows(d), cols(LANE), cols(LANE), rows(LANE), rows(LANE)] + specs,
        out_specs=[tiles(hq), rows(hq), tiles(hv)],
        out_shape=[jax.ShapeDtypeStruct((t // ATTN_TILE, hq, ATTN_TILE), BF16), jax.ShapeDtypeStruct((t, hq), BF16),
                   jax.ShapeDtypeStruct((t // ATTN_TILE, hv, ATTN_TILE), BF16)],
        compiler_params=_params("parallel"),
        name="mla_proj",
    )(x, *tables, *arrays)


def _mla_attn_kernel(q_ref, k_ref, vt_ref, *rest, tile, n_full):
    o_ref = rest[-1]
    half = tile // 2
    chains = [(g, c) for g in range(MLA_HEADS) for c in range(2)]
    qs = [q_ref[0, g * MLA_QK_PAD:(g + 1) * MLA_QK_PAD, c * half:(c + 1) * half] for g, c in chains]

    def scores(kb, q, mask=None):
        s = _dot(kb, q)
        return s if mask is None else jnp.where(mask, s, -jnp.inf)

    def accumulate(carry, s, vtb):
        m, acc = carry
        m_new = jnp.maximum(m, jnp.max(s, axis=0, keepdims=True))
        p = jnp.exp2(s - m_new)
        acc = jnp.exp2(m - m_new) * acc + _dot(vtb, p.astype(BF16))
        return m_new, acc

    def k_block(g, start, size):
        return k_ref[start:start + size, g * MLA_QK_PAD:(g + 1) * MLA_QK_PAD]

    def causal(keys, shift):
        row = lax.broadcasted_iota(jnp.int32, (keys, half), 0)
        col = lax.broadcasted_iota(jnp.int32, (keys, half), 1)
        return row <= col + shift

    def block_scores(j, n):
        g, c = chains[n]
        if j < n_full:
            return scores(k_block(g, j * tile, tile), qs[n])
        keys = (half, tile)[c]
        return scores(k_block(g, j * tile, keys), qs[n], causal(keys, c * half))

    def block_vt(j, n):
        g, c = chains[n]
        keys = tile if j < n_full else (half, tile)[c]
        return vt_ref[j, g * MLA_V_ROWS:(g + 1) * MLA_V_ROWS, :keys]

    init = (jnp.full((1, half), -jnp.inf, F32), jnp.zeros((MLA_V_ROWS, half), F32))
    carries = [init] * len(chains)
    ss = [block_scores(0, n) for n in range(len(chains))]
    for j in range(n_full):
        for n in range(len(chains)):
            carries[n] = accumulate(carries[n], ss[n], block_vt(j, n))
            ss[n] = block_scores(j + 1, n)

    base = n_full * tile if o_ref.shape[0] > tile else 0
    if o_ref.shape[0] > tile:
        o_ref[...] = jnp.zeros_like(o_ref)
    for n, (g, c) in enumerate(chains):
        _, acc = accumulate(carries[n], ss[n], block_vt(n_full, n))
        out = acc[:MLA_V] / acc[MLA_V:MLA_V + 1]
        o_ref[base + c * half:base + (c + 1) * half, g * MLA_V:(g + 1) * MLA_V] = out.T.astype(o_ref.dtype)


def _mla_attn(qt, k, vt, batch, seq, tile=ATTN_TILE):
    t = k.shape[0]
    nq = seq // tile
    out = None
    for i in reversed(range(nq)):
        kv_tiles = i + 1 if nq % (i + 1) == 0 else nq
        per_batch = nq // kv_tiles
        in_specs = [pl.BlockSpec((1, MLA_HEADS * MLA_QK_PAD, tile), lambda b, i=i: (b * nq + i, 0, 0)),
                    pl.BlockSpec((kv_tiles * tile, MLA_HEADS * MLA_QK_PAD), lambda b, n=per_batch: (b * n, 0)),
                    pl.BlockSpec((kv_tiles, MLA_HEADS * MLA_V_ROWS, tile), lambda b, n=per_batch: (b * n, 0, 0))]
        operands = [qt, k, vt]
        if out is None:
            out_spec = pl.BlockSpec((seq, MLA_HEADS * MLA_V), lambda b: (b, 0))
        else:
            out_spec = pl.BlockSpec((tile, MLA_HEADS * MLA_V), lambda b, i=i: (b * nq + i, 0))
            in_specs.append(pl.BlockSpec(memory_space=pl.ANY))
            operands.append(out)
        out = pl.pallas_call(
            functools.partial(_mla_attn_kernel, tile=tile, n_full=i),
            grid=(batch,),
            in_specs=in_specs,
            out_specs=out_spec,
            out_shape=jax.ShapeDtypeStruct((t, MLA_HEADS * MLA_V), BF16),
            input_output_aliases={3: 0} if len(operands) == 4 else {},
            compiler_params=_params("parallel"),
            name=f"mla_attn_q{i}",
        )(*operands)
    return out


def _log_sigmoid(x):
    return jnp.minimum(x, 0.0) - jnp.log1p(jnp.exp(-jnp.abs(x)))


def _bf16_parts(x, axis):
    parts = []
    for _ in range(3):
        p = x.astype(BF16).astype(F32)
        parts.append(p)
        x = x - p
    return jnp.concatenate(parts, axis=axis).astype(BF16)


def _mlstm_proj_kernel(x_ref, gpre_ref, wk_ref, wt_ref, wgt_ref, bgt_ref, eye_ref,
                       k_ref, qt_ref, vt_ref, sgt_ref, gb_ref, cc_ref):
    nq = MLSTM_HEADS * MLSTM_DK
    nv = MLSTM_HEADS * MLSTM_DV
    L = MLSTM_CHUNK
    tri = (lax.broadcasted_iota(jnp.int32, (L, L), 0) <= lax.broadcasted_iota(jnp.int32, (L, L), 1)).astype(BF16)
    is_input_gate = lax.broadcasted_iota(jnp.int32, (2 * MLSTM_HEADS, L), 0) < MLSTM_HEADS
    subs = _sub_tiles(x_ref.shape[0])
    hns = [_rms(x_ref[r, :], gpre_ref[...]).astype(BF16) for r in subs]
    gates = [_dot_nt(wgt_ref[...], hn) + bgt_ref[...] for hn in hns]
    projs = [_dot_nt(wt_ref[...], hns[0])]
    cums = [_dot(_bf16_parts(_log_sigmoid(g), axis=0), tri) for g in gates]
    projs += [_dot_nt(wt_ref[...], hn) for hn in hns[1:]]
    for r, g, cum in zip(subs, gates, cums):
        b_rows = cum[:8] + cum[8:16] + cum[16:]
        gb_ref[:, r] = jnp.where(is_input_gate, g, b_rows)
        c_rows = g - pltpu.roll(b_rows, MLSTM_HEADS, axis=0)
        cc_ref[r, :] = _dot_nt(eye_ref[...], _bf16_parts(c_rows, axis=1))
    for r, hn, proj_t in zip(subs, hns, projs):
        k_ref[r, :] = _dot(hn, wk_ref[...]).astype(BF16)
        qt_ref[:, r] = (proj_t[:nq] * MLSTM_DK ** -0.5).astype(BF16)
        vt_ref[:, r] = proj_t[nq:nq + nv].astype(BF16)
        sgt_ref[:, r] = jax.nn.sigmoid(proj_t[nq + nv:])


def _mlstm_proj(x, params, tm=BIG_ROW_TILE):
    assert SUB_ROWS == MLSTM_CHUNK
    t, d = x.shape
    nq = MLSTM_HEADS * MLSTM_DK
    nv = MLSTM_HEADS * MLSTM_DV
    ng = 2 * MLSTM_HEADS
    arrays, specs = _split(params)
    eye3 = jnp.tile(jnp.eye(MLSTM_CHUNK, dtype=BF16), (1, 3))
    rows = lambda n: pl.BlockSpec((tm, n), lambda i: (i, 0))
    cols = lambda n: pl.BlockSpec((n, tm), lambda i: (0, i))
    return pl.pallas_call(
        _mlstm_proj_kernel,
        grid=(t // tm,),
        in_specs=[rows(d)] + specs + [pl.BlockSpec(eye3.shape, lambda i: (0, 0), pipeline_mode=pl.Buffered(1))],
        out_specs=[rows(nq), cols(nq), cols(nv), cols(nv), cols(ng), rows(ng)],
        out_shape=[jax.ShapeDtypeStruct((t, nq), BF16), jax.ShapeDtypeStruct((nq, t), BF16),
                   jax.ShapeDtypeStruct((nv, t), BF16), jax.ShapeDtypeStruct((nv, t), F32),
                   jax.ShapeDtypeStruct((ng, t), F32), jax.ShapeDtypeStruct((t, ng), F32)],
        compiler_params=_params("parallel"),
        name="mlstm_proj",
    )(x, *arrays, eye3)


def _mlstm_scan_kernel(k_ref, qt_ref, vt_ref, sgt_ref, gb_ref, cc_ref, hn_ref, o_ref, state_ref, m_ref):
    L = MLSTM_CHUNK
    H, DK, DV = MLSTM_HEADS, MLSTM_DK, MLSTM_DV

    @pl.when(pl.program_id(1) == 0)
    def _():
        state_ref[...] = jnp.zeros_like(state_ref)
        m_ref[...] = jnp.zeros_like(m_ref)

    causal_t = (lax.broadcasted_iota(jnp.int32, (L, L), 0)
                <= lax.broadcasted_iota(jnp.int32, (L, L), 1))

    for ci in range(MLSTM_STEP_CHUNKS):
        t0 = ci * L
        gb = gb_ref[:, t0:t0 + L]
        cc = cc_ref[t0:t0 + L, :]
        ks = [k_ref[t0:t0 + L, h * DK:(h + 1) * DK] for h in range(H)]
        qts = [qt_ref[h * DK:(h + 1) * DK, t0:t0 + L] for h in range(H)]
        qk, inter, gates = [], [], []
        for h in range(H):
            qk.append(_dot(ks[h], qts[h]))
            inter.append(_dot(state_ref[h].astype(BF16), qts[h]))
            b_r = gb[H + h:H + h + 1, :]
            c_c = cc[:, h:h + 1]
            m = m_ref[h][:, 0:1]
            d = jnp.where(causal_t, b_r + c_c, -jnp.inf)
            g = b_r + m
            mt = jnp.maximum(g, jnp.max(d, axis=0, keepdims=True))
            gates.append((b_r, m, mt, jnp.exp(g - mt), jnp.exp(d - mt)))

        for h in range(H):
            b_r, m, mt, w_inter, w_intra = gates[h]
            li_r = gb[h:h + 1, :]
            s = qk[h] * w_intra
            vt = vt_ref[h * DV:(h + 1) * DV, t0:t0 + L]
            num = w_inter * inter[h][:DV] + _dot(vt, s.astype(BF16))
            den = w_inter * inter[h][DV:DV + 1] + jnp.sum(s, axis=0, keepdims=True)
            r_den = 1.0 / jnp.maximum(jnp.abs(den), jnp.exp(-mt))
            ms = jnp.mean(num * num, axis=0, keepdims=True) * (r_den * r_den)
            hcn = num * (r_den * lax.rsqrt(ms + NORM_EPS)) * hn_ref[h]
            out = sgt_ref[h * DV:(h + 1) * DV, t0:t0 + L] * hcn
            o_ref[t0:t0 + L, h * DV:(h + 1) * DV] = out.T.astype(o_ref.dtype)

            b_last = b_r[:, L - 1:L]
            m_new = mt[:, L - 1:L]
            w_k = jnp.exp(b_last - b_r + li_r - m_new)
            decay = jnp.exp(b_last + m - m_new)
            vt_ext = jnp.concatenate([(vt.astype(F32) * w_k).astype(BF16),
                                      jnp.broadcast_to(w_k, (8, L)).astype(BF16)], axis=0)
            state_ref[h] = decay * state_ref[h] + _dot(vt_ext, ks[h])
            m_ref[h] = jnp.broadcast_to(m_new, (1, LANE))


def _mlstm_scan(k, qt, vt, sgt, gb, cc, hnorm, batch, seq):
    t = k.shape[0]
    rows_per_step = MLSTM_CHUNK * MLSTM_STEP_CHUNKS
    steps = seq // rows_per_step
    nq = MLSTM_HEADS * MLSTM_DK
    nv = MLSTM_HEADS * MLSTM_DV
    arrays, specs = _split([hnorm])
    rows = lambda n: pl.BlockSpec((rows_per_step, n), lambda b, c: (b * steps + c, 0))
    cols = lambda n: pl.BlockSpec((n, rows_per_step), lambda b, c: (0, b * steps + c))
    return pl.pallas_call(
        _mlstm_scan_kernel,
        grid=(batch, steps),
        in_specs=[rows(nq), cols(nq), cols(nv), cols(nv), cols(2 * MLSTM_HEADS), rows(2 * MLSTM_HEADS)] + specs,
        out_specs=rows(nv),
        out_shape=jax.ShapeDtypeStruct((t, nv), BF16),
        scratch_shapes=[pltpu.VMEM((MLSTM_HEADS, MLSTM_DV + 8, MLSTM_DK), F32),
                        pltpu.VMEM((MLSTM_HEADS, 1, LANE), F32)],
        compiler_params=_params("parallel", "arbitrary"),
        name="mlstm_scan",
    )(k, qt, vt, sgt, gb, cc, *arrays)


def _mix_out_mem_xattn_kernel(a_ref, x_ref, kv_ref, wmix_ref, gmix_ref, gq_ref, wq_ref, wo_ref, gpost_ref,
                              o_ref, oc_ref):
    subs = _sub_tiles(x_ref.shape[0])
    nk = MEM_HEADS * MEM_DH
    heads = [(h * MEM_DH, (h + 1) * MEM_DH) for h in range(MEM_HEADS)]

    def mix_stage(r, st):
        st["x"] = x_ref[r, :] + _rms(_dot(a_ref[r, :], wmix_ref[...]), gmix_ref[...])

    def q_stage(r, st):
        hn = _rms(st["x"], gq_ref[...]).astype(BF16)
        st["q"] = (_dot(hn, wq_ref[...]) * MEM_DH ** -0.5).astype(BF16)

    def score_stage(r, st):
        st["s"] = [_dot_nt(st["q"][:, lo:hi], kv_ref[:, lo:hi]) for lo, hi in heads]

    def pv_stage(r, st):
        for (lo, hi), s in zip(heads, st["s"]):
            e = jnp.exp(s - jnp.max(s, axis=-1, keepdims=True))
            p = e / jnp.sum(e, axis=-1, keepdims=True)
            oc_ref[r, lo:hi] = _dot(p.astype(BF16), kv_ref[:, nk + lo:nk + hi]).astype(BF16)

    def out_stage(r, st):
        o_ref[r, :] = st["x"] + _rms(_dot(oc_ref[r, :], wo_ref[...]), gpost_ref[...])

    _wavefront(subs, [mix_stage, q_stage, score_stage, pv_stage, out_stage])


def _mix_out_mem_xattn(a, x, kv, layer, params, batch, seq, tm=BIG_ROW_TILE):
    t, d = x.shape
    nt = seq // tm
    arrays, specs = _split(params)
    rows = lambda n: pl.BlockSpec((tm, n), lambda b, i: (b * nt + i, 0))
    kv_spec = pl.BlockSpec((None, N_MEM, kv.shape[2]), lambda b, i: (layer, b, 0))
    return pl.pallas_call(
        _mix_out_mem_xattn_kernel,
        grid=(batch, nt),
        in_specs=[rows(a.shape[1]), rows(d), kv_spec] + specs,
        out_specs=rows(d),
        out_shape=jax.ShapeDtypeStruct((t, d), F32),
        scratch_shapes=[pltpu.VMEM((tm, MEM_HEADS * MEM_DH), BF16)],
        compiler_params=_params("parallel", "parallel"),
        name="mix_out_mem_xattn",
    )(a, x, kv, *arrays)


def _swiglu_kernel(x_ref, gpre_ref, wgu_ref, wd_ref, gpost_ref, o_ref, act_ref):
    subs = _sub_tiles(x_ref.shape[0])
    hn = [_rms(x_ref[r, :], gpre_ref[...]).astype(BF16) for r in subs]
    for c in range(D_FF // FF_CHUNK):
        lo, hi = c * FF_CHUNK, (c + 1) * FF_CHUNK
        for r, h in zip(subs, hn):
            gate = _dot(h, wgu_ref[:, lo:hi])
            up = _dot(h, wgu_ref[:, D_FF + lo:D_FF + hi])
            act_ref[r, lo:hi] = (gate * jax.nn.sigmoid(gate) * up).astype(BF16)
    down = [_dot(act_ref[r, :], wd_ref[...]) for r in subs]
    for r, y in zip(subs, down):
        o_ref[r, :] = x_ref[r, :] + _rms(y, gpost_ref[...])


def _swiglu(x, params, tm=BIG_ROW_TILE):
    t, d = x.shape
    arrays, specs = _split(params)
    return pl.pallas_call(
        _swiglu_kernel,
        grid=(t // tm,),
        in_specs=[pl.BlockSpec((tm, d), lambda i: (i, 0))] + specs,
        out_specs=pl.BlockSpec((tm, d), lambda i: (i, 0)),
        out_shape=jax.ShapeDtypeStruct((t, d), F32),
        scratch_shapes=[pltpu.VMEM((tm, D_FF), BF16)],
        compiler_params=_params("parallel"),
        name="swiglu",
    )(x, *arrays)


def _pad_last(w, n):
    return jnp.pad(w, [(0, 0)] * (w.ndim - 1) + [(0, n - w.shape[-1])])


def _swap_halves(w):
    half = w.shape[-1] // 2
    return jnp.concatenate([w[..., half:], w[..., :half]], axis=-1)


def _transpose(w):
    return jnp.swapaxes(w, -1, -2)


def _mla_weights(w_in, w_uq, w_ukv):
    n = w_in.shape[0]
    r = MLA_Q_RANK + MLA_KV_RANK
    w_kr = w_in[..., r:]
    win = jnp.concatenate([w_in[..., :r], _pad_last(w_kr, LANE), _pad_last(_swap_halves(w_kr), LANE)], axis=-1)
    uq = w_uq.reshape(n, MLA_Q_RANK, MLA_HEADS, MLA_NOPE + MLA_ROPE)
    wqn = uq[..., :MLA_NOPE].reshape(n, MLA_Q_RANK, MLA_HEADS * LANE)
    rope = uq[..., MLA_NOPE:]
    wqr = rope.reshape(n, MLA_Q_RANK, MLA_HEADS * MLA_ROPE)
    wqs = _swap_halves(rope).reshape(n, MLA_Q_RANK, MLA_HEADS * MLA_ROPE)
    ukv = w_ukv.reshape(n, MLA_KV_RANK, MLA_HEADS, 2, MLA_NOPE)
    wk = ukv[:, :, :, 0, :].reshape(n, MLA_KV_RANK, MLA_HEADS * MLA_NOPE)
    wvt = _transpose(ukv[:, :, :, 1, :].reshape(n, MLA_KV_RANK, MLA_HEADS * MLA_V))
    return tuple(w.astype(BF16) for w in (win, _transpose(wqn), _transpose(wqr), _transpose(wqs), wk, wvt))


def _mlstm_weights(w_in, b_gates):
    nq = MLSTM_HEADS * MLSTM_DK
    nv = MLSTM_HEADS * MLSTM_DV
    w_q, w_k = w_in[..., :nq], w_in[..., nq:2 * nq]
    w_vo = w_in[..., 2 * nq:2 * nq + 2 * nv]
    w_g = w_in[..., 2 * nq + 2 * nv:]
    w_t = _transpose(jnp.concatenate([w_q, w_vo], axis=-1)).astype(BF16)
    return w_k.astype(BF16), w_t, _transpose(w_g).astype(BF16), b_gates[:, :, None]


def kernel(x, mem, positions, mla_w_in, mla_q_norm, mla_w_uq, mla_kv_norm, mla_w_ukv, mla_w_o, mlstm_w_in, mlstm_b_gates, mlstm_head_norm, mlstm_w_o, norm_mix_pre, norm_mix_post, norm_mem_q, norm_mem_kv, norm_mem_post, norm_ffn_pre, norm_ffn_post, mem_w_q, mem_w_kv, mem_w_o, ffn_w_gate_up, ffn_w_down):
    batch, seq, d = x.shape
    t = batch * seq
    xf = x.reshape(t, d)
    memf = mem.reshape(batch * mem.shape[1], d)
    tables = _rope_tables(positions.reshape(1, t))

    gain = lambda g: g[:, None, :]
    bf16 = lambda w: w.astype(BF16)
    g_mix_pre, g_mix_post = gain(norm_mix_pre), gain(norm_mix_post)
    g_mem_q, g_mem_kv, g_mem_post = gain(norm_mem_q), gain(norm_mem_kv), gain(norm_mem_post)
    g_ffn_pre, g_ffn_post = gain(norm_ffn_pre), gain(norm_ffn_post)
    mla_w = _mla_weights(mla_w_in, mla_w_uq, mla_w_ukv)
    mla_gq, mla_gkv, mla_wo = gain(mla_q_norm), gain(mla_kv_norm), bf16(mla_w_o)
    mlstm_w = _mlstm_weights(mlstm_w_in, mlstm_b_gates)
    mlstm_hn, mlstm_wo = mlstm_head_norm[..., None], bf16(mlstm_w_o)
    mem_wq, mem_wkv, mem_wo = bf16(mem_w_q), bf16(mem_w_kv), bf16(mem_w_o)
    ffn_wgu, ffn_wd = bf16(ffn_w_gate_up), bf16(ffn_w_down)
    mem_kv = _mem_kv(memf, g_mem_kv, mem_wkv)

    for i in range(DEPTH):
        j = i // 2
        if i % 2 == 0:
            win, wqn, wqr, wqs, wk, wvt = mla_w
            qt, k, vt = _mla_proj(xf, tables, [(g_mix_pre, i), (win, j), (mla_gq, j), (wqn, j), (wqr, j),
                                               (wqs, j), (mla_gkv, j), (wk, j), (wvt, j)])
            a = _mla_attn(qt, k, vt, batch, seq)
            w_mix = mla_wo
        else:
            wk, wt, wgt, bgt = mlstm_w
            k, qt, vt, sgt, gb, cc = _mlstm_proj(xf, [(g_mix_pre, i), (wk, j), (wt, j), (wgt, j), (bgt, j)])
            a = _mlstm_scan(k, qt, vt, sgt, gb, cc, (mlstm_hn, j), batch, seq)
            w_mix = mlstm_wo

        xf = _mix_out_mem_xattn(a, xf, mem_kv, i, [(w_mix, j), (g_mix_post, i), (g_mem_q, i), (mem_wq, i),
                                                   (mem_wo, i), (g_mem_post, i)], batch, seq)
        xf = _swiglu(xf, [(g_ffn_pre, i), (ffn_wgu, i), (ffn_wd, i), (g_ffn_post, i)])
    return xf.reshape(batch, seq, d)
```

```python
import functools

import jax
import jax.numpy as jnp
from jax import lax
from jax.experimental import pallas as pl
from jax.experimental.pallas import tpu as pltpu

F32 = jnp.float32
BF16 = jnp.bfloat16

DEPTH = 4
NORM_EPS = 1e-6

MLA_HEADS = 8
MLA_NOPE = 128
MLA_ROPE = 64
MLA_V = 128
MLA_V_ROWS = MLA_V + 16
MLA_Q_RANK = 384
MLA_KV_RANK = 256
MLA_QK_PAD = 256
ROPE_BASE = 10000.0

MLSTM_HEADS = 4
MLSTM_DV = 256
MLSTM_DK = 128
MLSTM_CHUNK = 256
MLSTM_STEP_CHUNKS = 8

N_MEM = 256
MEM_HEADS = 4
MEM_DH = 256

D_FF = 2816
FF_CHUNK = 256

ROW_TILE = 512
BIG_ROW_TILE = 1024
SUB_ROWS = 256
ATTN_TILE = 512
MLA_Q_SCALE = (MLA_NOPE + MLA_ROPE) ** -0.5 * 1.4426950408889634
LANE = 128
VMEM_LIMIT = 56 * 1024 * 1024


def _rms(x, g):
    ms = jnp.mean(x * x, axis=-1, keepdims=True)
    return x * lax.rsqrt(ms + NORM_EPS) * g


def _dot(a, b):
    return jnp.dot(a, b, preferred_element_type=F32)


def _dot_nt(a, b):
    return lax.dot_general(a, b, (((1,), (1,)), ((), ())), preferred_element_type=F32)


def _layer_spec(w, layer):
    nd = w.ndim - 1
    return pl.BlockSpec((None,) + w.shape[1:], lambda *_: (layer,) + (0,) * nd, pipeline_mode=pl.Buffered(1))


def _split(params):
    return [w for w, _ in params], [_layer_spec(w, layer) for w, layer in params]


def _sub_tiles(rows):
    return [pl.ds(i * SUB_ROWS, SUB_ROWS) for i in range(rows // SUB_ROWS)]


def _wavefront(subs, stages):
    states = [{} for _ in subs]
    for t in range(len(subs) + len(stages) - 1):
        for i, r in reversed(list(enumerate(subs))):
            if 0 <= t - i < len(stages):
                stages[t - i](r, states[i])


def _params(*sem):
    return pltpu.CompilerParams(dimension_semantics=sem, vmem_limit_bytes=VMEM_LIMIT)


def _rope_table_kernel(pos_ref, freq_ref, cos_t_ref, sin_t_ref, cos_ref, sin_ref):
    ang = freq_ref[...] * pos_ref[...].astype(F32)
    c, s = jnp.cos(ang), jnp.sin(ang)
    zeros = jnp.zeros((LANE - MLA_ROPE, ang.shape[1]), F32)
    cos_t = jnp.concatenate([c, c, zeros], axis=0)
    sin_t = jnp.concatenate([-s, s, zeros], axis=0)
    cos_t_ref[...] = cos_t
    sin_t_ref[...] = sin_t
    cos_ref[...] = cos_t.T
    sin_ref[...] = sin_t.T


def _rope_tables(pos_row):
    t = pos_row.shape[1]
    half = MLA_ROPE // 2
    inv_freq = ROPE_BASE ** (-jnp.arange(0, MLA_ROPE, 2, dtype=F32) / MLA_ROPE)
    tm = 1024
    return pl.pallas_call(
        _rope_table_kernel,
        grid=(t // tm,),
        in_specs=[pl.BlockSpec((1, tm), lambda i: (0, i)), pl.BlockSpec((half, 1), lambda i: (0, 0))],
        out_specs=[pl.BlockSpec((LANE, tm), lambda i: (0, i))] * 2 + [pl.BlockSpec((tm, LANE), lambda i: (i, 0))] * 2,
        out_shape=[jax.ShapeDtypeStruct((LANE, t), F32)] * 2 + [jax.ShapeDtypeStruct((t, LANE), F32)] * 2,
        compiler_params=_params("parallel"),
        name="rope_tables",
    )(pos_row, inv_freq[:, None])


def _mem_kv_kernel(x_ref, g_ref, w_ref, o_ref):
    hn = _rms(x_ref[...], g_ref[...]).astype(BF16)
    o_ref[...] = _dot(hn, w_ref[...]).astype(o_ref.dtype)


def _mem_kv(x, g, w, tm=ROW_TILE):
    t, d = x.shape
    layers, _, n = w.shape
    return pl.pallas_call(
        _mem_kv_kernel,
        grid=(layers, t // tm),
        in_specs=[pl.BlockSpec((tm, d), lambda l, i: (i, 0)),
                  pl.BlockSpec((None, 1, d), lambda l, i: (l, 0, 0)),
                  pl.BlockSpec((None, d, n), lambda l, i: (l, 0, 0))],
        out_specs=pl.BlockSpec((None, tm, n), lambda l, i: (l, i, 0)),
        out_shape=jax.ShapeDtypeStruct((layers, t, n), BF16),
        compiler_params=_params("parallel", "parallel"),
        name="mem_kv",
    )(x, g, w)


def _mla_proj_kernel(x_ref, cos_t_ref, sin_t_ref, cos_ref, sin_ref, gpre_ref, win_ref, gq_ref, wqn_ref,
                     wqr_ref, wqs_ref, gkv_ref, wk_ref, wvt_ref, qt_ref, k_ref, vt_ref):
    subs = _sub_tiles(x_ref.shape[0])
    hns = [_rms(x_ref[r, :], gpre_ref[...]).astype(BF16) for r in subs]
    projs = [_dot(hn, win_ref[...]) for hn in hns]
    for r, proj in zip(subs, projs):
        ti, tl = r.start // ATTN_TILE, pl.ds(r.start % ATTN_TILE, r.size)
        cq = proj[:, :MLA_Q_RANK]
        ckv = proj[:, MLA_Q_RANK:MLA_Q_RANK + MLA_KV_RANK]
        kr = proj[:, MLA_Q_RANK + MLA_KV_RANK:MLA_Q_RANK + MLA_KV_RANK + LANE]
        krs = proj[:, MLA_Q_RANK + MLA_KV_RANK + LANE:]
        k_rope = (kr * cos_ref[r, :] + krs * sin_ref[r, :]).astype(BF16)

        cqn = _rms(cq, gq_ref[...]).astype(BF16)
        q_nope = _dot_nt(wqn_ref[...], cqn)
        q_r = _dot_nt(wqr_ref[...], cqn)
        q_s = _dot_nt(wqs_ref[...], cqn)
        ckvn = _rms(ckv, gkv_ref[...]).astype(BF16)
        k_nope = _dot(ckvn, wk_ref[...])
        v_t = _dot_nt(wvt_ref[...], ckvn).astype(BF16)
        ones = jnp.ones((MLA_V_ROWS - MLA_V, v_t.shape[1]), BF16)
        for h in range(MLA_HEADS):
            vt_ref[ti, h * MLA_V_ROWS:h * MLA_V_ROWS + MLA_V, tl] = v_t[h * MLA_V:(h + 1) * MLA_V]
            vt_ref[ti, h * MLA_V_ROWS + MLA_V:(h + 1) * MLA_V_ROWS, tl] = ones

        cos_t = cos_t_ref[:MLA_ROPE, r] * MLA_Q_SCALE
        sin_t = sin_t_ref[:MLA_ROPE, r] * MLA_Q_SCALE
        zeros = jnp.zeros((MLA_QK_PAD - MLA_NOPE - MLA_ROPE, cos_t.shape[1]), BF16)
        for h in range(MLA_HEADS):
            lo, hi = h * LANE, (h + 1) * LANE
            rlo, rhi = h * MLA_ROPE, (h + 1) * MLA_ROPE
            base = h * MLA_QK_PAD
            qt_ref[ti, base:base + MLA_NOPE, tl] = (q_nope[lo:hi] * MLA_Q_SCALE).astype(BF16)
            qt_ref[ti, base + MLA_NOPE:base + MLA_NOPE + MLA_ROPE, tl] = (
                q_r[rlo:rhi] * cos_t + q_s[rlo:rhi] * sin_t).astype(BF16)
            qt_ref[ti, base + MLA_NOPE + MLA_ROPE:base + MLA_QK_PAD, tl] = zeros
            k_ref[r, base:base + LANE] = k_nope[:, lo:hi].astype(BF16)
            k_ref[r, base + LANE:base + 2 * LANE] = k_rope


def _mla_proj(x, tables, params, tm=BIG_ROW_TILE):
    t, d = x.shape
    hq = MLA_HEADS * MLA_QK_PAD
    hv = MLA_HEADS * MLA_V_ROWS
    arrays, specs = _split(params)
    rows = lambda n: pl.BlockSpec((tm, n), lambda i: (i, 0))
    cols = lambda n: pl.BlockSpec((n, tm), lambda i: (0, i))
    tiles = lambda n: pl.BlockSpec((tm // ATTN_TILE, n, ATTN_TILE), lambda i: (i, 0, 0))
    return pl.pallas_call(
        _mla_proj_kernel,
        grid=(t // tm,),
        in_specs=[rows(d), cols(LANE), cols(LANE), rows(LANE), rows(LANE)] + specs,
        out_specs=[tiles(hq), rows(hq), tiles(hv)],
        out_shape=[jax.ShapeDtypeStruct((t // ATTN_TILE, hq, ATTN_TILE), BF16), jax.ShapeDtypeStruct((t, hq), BF16),
                   jax.ShapeDtypeStruct((t // ATTN_TILE, hv, ATTN_TILE), BF16)],
        compiler_params=_params("parallel"),
        name="mla_proj",
    )(x, *tables, *arrays)


def _mla_attn_kernel(q_ref, k_ref, vt_ref, *rest, tile, n_full):
    o_ref = rest[-1]
    half = tile // 2
    chains = [(g, c) for g in range(MLA_HEADS) for c in range(2)]
    qs = [q_ref[0, g * MLA_QK_PAD:(g + 1) * MLA_QK_PAD, c * half:(c + 1) * half] for g, c in chains]

    def scores(kb, q, mask=None):
        s = _dot(kb, q)
        return s if mask is None else jnp.where(mask, s, -jnp.inf)

    def accumulate(carry, s, vtb):
        m, acc = carry
        m_new = jnp.maximum(m, jnp.max(s, axis=0, keepdims=True))
        p = jnp.exp2(s - m_new)
        acc = jnp.exp2(m - m_new) * acc + _dot(vtb, p.astype(BF16))
        return m_new, acc

    def k_block(g, start, size):
        return k_ref[start:start + size, g * MLA_QK_PAD:(g + 1) * MLA_QK_PAD]

    def causal(keys, shift):
        row = lax.broadcasted_iota(jnp.int32, (keys, half), 0)
        col = lax.broadcasted_iota(jnp.int32, (keys, half), 1)
        return row <= col + shift

    def block_scores(j, n):
        g, c = chains[n]
        if j < n_full:
            return scores(k_block(g, j * tile, tile), qs[n])
        keys = (half, tile)[c]
        return scores(k_block(g, j * tile, keys), qs[n], causal(keys, c * half))

    def block_vt(j, n):
        g, c = chains[n]
        keys = tile if j < n_full else (half, tile)[c]
        return vt_ref[j, g * MLA_V_ROWS:(g + 1) * MLA_V_ROWS, :keys]

    init = (jnp.full((1, half), -jnp.inf, F32), jnp.zeros((MLA_V_ROWS, half), F32))
    carries = [init] * len(chains)
    ss = [block_scores(0, n) for n in range(len(chains))]
    for j in range(n_full):
        for n in range(len(chains)):
            carries[n] = accumulate(carries[n], ss[n], block_vt(j, n))
            ss[n] = block_scores(j + 1, n)

    base = n_full * tile if o_ref.shape[0] > tile else 0
    if o_ref.shape[0] > tile:
        o_ref[...] = jnp.zeros_like(o_ref)
    for n, (g, c) in enumerate(chains):
        _, acc = accumulate(carries[n], ss[n], block_vt(n_full, n))
        out = acc[:MLA_V] / acc[MLA_V:MLA_V + 1]
        o_ref[base + c * half:base + (c + 1) * half, g * MLA_V:(g + 1) * MLA_V] = out.T.astype(o_ref.dtype)


def _mla_attn(qt, k, vt, batch, seq, tile=ATTN_TILE):
    t = k.shape[0]
    nq = seq // tile
    out = None
    for i in reversed(range(nq)):
        kv_tiles = i + 1 if nq % (i + 1) == 0 else nq
        per_batch = nq // kv_tiles
        in_specs = [pl.BlockSpec((1, MLA_HEADS * MLA_QK_PAD, tile), lambda b, i=i: (b * nq + i, 0, 0)),
                    pl.BlockSpec((kv_tiles * tile, MLA_HEADS * MLA_QK_PAD), lambda b, n=per_batch: (b * n, 0)),
                    pl.BlockSpec((kv_tiles, MLA_HEADS * MLA_V_ROWS, tile), lambda b, n=per_batch: (b * n, 0, 0))]
        operands = [qt, k, vt]
        if out is None:
            out_spec = pl.BlockSpec((seq, MLA_HEADS * MLA_V), lambda b: (b, 0))
        else:
            out_spec = pl.BlockSpec((tile, MLA_HEADS * MLA_V), lambda b, i=i: (b * nq + i, 0))
            in_specs.append(pl.BlockSpec(memory_space=pl.ANY))
            operands.append(out)
        out = pl.pallas_call(
            functools.partial(_mla_attn_kernel, tile=tile, n_full=i),
            grid=(batch,),
            in_specs=in_specs,
            out_specs=out_spec,
            out_shape=jax.ShapeDtypeStruct((t, MLA_HEADS * MLA_V), BF16),
            input_output_aliases={3: 0} if len(operands) == 4 else {},
            compiler_params=_params("parallel"),
            name=f"mla_attn_q{i}",
        )(*operands)
    return out


def _log_sigmoid(x):
    return jnp.minimum(x, 0.0) - jnp.log1p(jnp.exp(-jnp.abs(x)))


def _bf16_parts(x, axis):
    parts = []
    for _ in range(3):
        p = x.astype(BF16).astype(F32)
        parts.append(p)
        x = x - p
    return jnp.concatenate(parts, axis=axis).astype(BF16)


def _mlstm_proj_kernel(x_ref, gpre_ref, wk_ref, wt_ref, wgt_ref, bgt_ref, eye_ref,
                       k_ref, qt_ref, vt_ref, sgt_ref, gb_ref, cc_ref):
    nq = MLSTM_HEADS * MLSTM_DK
    nv = MLSTM_HEADS * MLSTM_DV
    L = MLSTM_CHUNK
    tri = (lax.broadcasted_iota(jnp.int32, (L, L), 0) <= lax.broadcasted_iota(jnp.int32, (L, L), 1)).astype(BF16)
    is_input_gate = lax.broadcasted_iota(jnp.int32, (2 * MLSTM_HEADS, L), 0) < MLSTM_HEADS
    subs = _sub_tiles(x_ref.shape[0])
    hns = [_rms(x_ref[r, :], gpre_ref[...]).astype(BF16) for r in subs]
    gates = [_dot_nt(wgt_ref[...], hn) + bgt_ref[...] for hn in hns]
    projs = [_dot_nt(wt_ref[...], hns[0])]
    cums = [_dot(_bf16_parts(_log_sigmoid(g), axis=0), tri) for g in gates]
    projs += [_dot_nt(wt_ref[...], hn) for hn in hns[1:]]
    for r, g, cum in zip(subs, gates, cums):
        b_rows = cum[:8] + cum[8:16] + cum[16:]
        gb_ref[:, r] = jnp.where(is_input_gate, g, b_rows)
        c_rows = g - pltpu.roll(b_rows, MLSTM_HEADS, axis=0)
        cc_ref[r, :] = _dot_nt(eye_ref[...], _bf16_parts(c_rows, axis=1))
    for r, hn, proj_t in zip(subs, hns, projs):
        k_ref[r, :] = _dot(hn, wk_ref[...]).astype(BF16)
        qt_ref[:, r] = (proj_t[:nq] * MLSTM_DK ** -0.5).astype(BF16)
        vt_ref[:, r] = proj_t[nq:nq + nv].astype(BF16)
        sgt_ref[:, r] = jax.nn.sigmoid(proj_t[nq + nv:])


def _mlstm_proj(x, params, tm=BIG_ROW_TILE):
    assert SUB_ROWS == MLSTM_CHUNK
    t, d = x.shape
    nq = MLSTM_HEADS * MLSTM_DK
    nv = MLSTM_HEADS * MLSTM_DV
    ng = 2 * MLSTM_HEADS
    arrays, specs = _split(params)
    eye3 = jnp.tile(jnp.eye(MLSTM_CHUNK, dtype=BF16), (1, 3))
    rows = lambda n: pl.BlockSpec((tm, n), lambda i: (i, 0))
    cols = lambda n: pl.BlockSpec((n, tm), lambda i: (0, i))
    return pl.pallas_call(
        _mlstm_proj_kernel,
        grid=(t // tm,),
        in_specs=[rows(d)] + specs + [pl.BlockSpec(eye3.shape, lambda i: (0, 0), pipeline_mode=pl.Buffered(1))],
        out_specs=[rows(nq), cols(nq), cols(nv), cols(nv), cols(ng), rows(ng)],
        out_shape=[jax.ShapeDtypeStruct((t, nq), BF16), jax.ShapeDtypeStruct((nq, t), BF16),
                   jax.ShapeDtypeStruct((nv, t), BF16), jax.ShapeDtypeStruct((nv, t), F32),
                   jax.ShapeDtypeStruct((ng, t), F32), jax.ShapeDtypeStruct((t, ng), F32)],
        compiler_params=_params("parallel"),
        name="mlstm_proj",
    )(x, *arrays, eye3)


def _mlstm_scan_kernel(k_ref, qt_ref, vt_ref, sgt_ref, gb_ref, cc_ref, hn_ref, o_ref, state_ref, m_ref):
    L = MLSTM_CHUNK
    H, DK, DV = MLSTM_HEADS, MLSTM_DK, MLSTM_DV

    @pl.when(pl.program_id(1) == 0)
    def _():
        state_ref[...] = jnp.zeros_like(state_ref)
        m_ref[...] = jnp.zeros_like(m_ref)

    causal_t = (lax.broadcasted_iota(jnp.int32, (L, L), 0)
                <= lax.broadcasted_iota(jnp.int32, (L, L), 1))

    for ci in range(MLSTM_STEP_CHUNKS):
        t0 = ci * L
        gb = gb_ref[:, t0:t0 + L]
        cc = cc_ref[t0:t0 + L, :]
        ks = [k_ref[t0:t0 + L, h * DK:(h + 1) * DK] for h in range(H)]
        qts = [qt_ref[h * DK:(h + 1) * DK, t0:t0 + L] for h in range(H)]
        qk, inter, gates = [], [], []
        for h in range(H):
            qk.append(_dot(ks[h], qts[h]))
            inter.append(_dot(state_ref[h].astype(BF16), qts[h]))
            b_r = gb[H + h:H + h + 1, :]
            c_c = cc[:, h:h + 1]
            m = m_ref[h][:, 0:1]
            d = jnp.where(causal_t, b_r + c_c, -jnp.inf)
            g = b_r + m
            mt = jnp.maximum(g, jnp.max(d, axis=0, keepdims=True))
            gates.append((b_r, m, mt, jnp.exp(g - mt), jnp.exp(d - mt)))

        for h in range(H):
            b_r, m, mt, w_inter, w_intra = gates[h]
            li_r = gb[h:h + 1, :]
            s = qk[h] * w_intra
            vt = vt_ref[h * DV:(h + 1) * DV, t0:t0 + L]
            num = w_inter * inter[h][:DV] + _dot(vt, s.astype(BF16))
            den = w_inter * inter[h][DV:DV + 1] + jnp.sum(s, axis=0, keepdims=True)
            r_den = 1.0 / jnp.maximum(jnp.abs(den), jnp.exp(-mt))
            ms = jnp.mean(num * num, axis=0, keepdims=True) * (r_den * r_den)
            hcn = num * (r_den * lax.rsqrt(ms + NORM_EPS)) * hn_ref[h]
            out = sgt_ref[h * DV:(h + 1) * DV, t0:t0 + L] * hcn
            o_ref[t0:t0 + L, h * DV:(h + 1) * DV] = out.T.astype(o_ref.dtype)

            b_last = b_r[:, L - 1:L]
            m_new = mt[:, L - 1:L]
            w_k = jnp.exp(b_last - b_r + li_r - m_new)
            decay = jnp.exp(b_last + m - m_new)
            vt_ext = jnp.concatenate([(vt.astype(F32) * w_k).astype(BF16),
                                      jnp.broadcast_to(w_k, (8, L)).astype(BF16)], axis=0)
            state_ref[h] = decay * state_ref[h] + _dot(vt_ext, ks[h])
            m_ref[h] = jnp.broadcast_to(m_new, (1, LANE))


def _mlstm_scan(k, qt, vt, sgt, gb, cc, hnorm, batch, seq):
    t = k.shape[0]
    rows_per_step = MLSTM_CHUNK * MLSTM_STEP_CHUNKS
    steps = seq // rows_per_step
    nq = MLSTM_HEADS * MLSTM_DK
    nv = MLSTM_HEADS * MLSTM_DV
    arrays, specs = _split([hnorm])
    rows = lambda n: pl.BlockSpec((rows_per_step, n), lambda b, c: (b * steps + c, 0))
    cols = lambda n: pl.BlockSpec((n, rows_per_step), lambda b, c: (0, b * steps + c))
    return pl.pallas_call(
        _mlstm_scan_kernel,
        grid=(batch, steps),
        in_specs=[rows(nq), cols(nq), cols(nv), cols(nv), cols(2 * MLSTM_HEADS), rows(2 * MLSTM_HEADS)] + specs,
        out_specs=rows(nv),
        out_shape=jax.ShapeDtypeStruct((t, nv), BF16),
        scratch_shapes=[pltpu.VMEM((MLSTM_HEADS, MLSTM_DV + 8, MLSTM_DK), F32),
                        pltpu.VMEM((MLSTM_HEADS, 1, LANE), F32)],
        compiler_params=_params("parallel", "arbitrary"),
        name="mlstm_scan",
    )(k, qt, vt, sgt, gb, cc, *arrays)


def _mix_out_mem_xattn_kernel(a_ref, x_ref, kv_ref, wmix_ref, gmix_ref, gq_ref, wq_ref, wo_ref, gpost_ref,
                              o_ref, oc_ref):
    subs = _sub_tiles(x_ref.shape[0])
    nk = MEM_HEADS * MEM_DH
    heads = [(h * MEM_DH, (h + 1) * MEM_DH) for h in range(MEM_HEADS)]

    def mix_stage(r, st):
        st["x"] = x_ref[r, :] + _rms(_dot(a_ref[r, :], wmix_ref[...]), gmix_ref[...])

    def q_stage(r, st):
        hn = _rms(st["x"], gq_ref[...]).astype(BF16)
        st["q"] = (_dot(hn, wq_ref[...]) * MEM_DH ** -0.5).astype(BF16)

    def score_stage(r, st):
        st["s"] = [_dot_nt(st["q"][:, lo:hi], kv_ref[:, lo:hi]) for lo, hi in heads]

    def pv_stage(r, st):
        for (lo, hi), s in zip(heads, st["s"]):
            e = jnp.exp(s - jnp.max(s, axis=-1, keepdims=True))
            p = e / jnp.sum(e, axis=-1, keepdims=True)
            oc_ref[r, lo:hi] = _dot(p.astype(BF16), kv_ref[:, nk + lo:nk + hi]).astype(BF16)

    def out_stage(r, st):
        o_ref[r, :] = st["x"] + _rms(_dot(oc_ref[r, :], wo_ref[...]), gpost_ref[...])

    _wavefront(subs, [mix_stage, q_stage, score_stage, pv_stage, out_stage])


def _mix_out_mem_xattn(a, x, kv, layer, params, batch, seq, tm=BIG_ROW_TILE):
    t, d = x.shape
    nt = seq // tm
    arrays, specs = _split(params)
    rows = lambda n: pl.BlockSpec((tm, n), lambda b, i: (b * nt + i, 0))
    kv_spec = pl.BlockSpec((None, N_MEM, kv.shape[2]), lambda b, i: (layer, b, 0))
    return pl.pallas_call(
        _mix_out_mem_xattn_kernel,
        grid=(batch, nt),
        in_specs=[rows(a.shape[1]), rows(d), kv_spec] + specs,
        out_specs=rows(d),
        out_shape=jax.ShapeDtypeStruct((t, d), F32),
        scratch_shapes=[pltpu.VMEM((tm, MEM_HEADS * MEM_DH), BF16)],
        compiler_params=_params("parallel", "parallel"),
        name="mix_out_mem_xattn",
    )(a, x, kv, *arrays)


def _swiglu_kernel(x_ref, gpre_ref, wgu_ref, wd_ref, gpost_ref, o_ref, act_ref):
    subs = _sub_tiles(x_ref.shape[0])
    hn = [_rms(x_ref[r, :], gpre_ref[...]).astype(BF16) for r in subs]
    for c in range(D_FF // FF_CHUNK):
        lo, hi = c * FF_CHUNK, (c + 1) * FF_CHUNK
        for r, h in zip(subs, hn):
            gate = _dot(h, wgu_ref[:, lo:hi])
            up = _dot(h, wgu_ref[:, D_FF + lo:D_FF + hi])
            act_ref[r, lo:hi] = (gate * jax.nn.sigmoid(gate) * up).astype(BF16)
    down = [_dot(act_ref[r, :], wd_ref[...]) for r in subs]
    for r, y in zip(subs, down):
        o_ref[r, :] = x_ref[r, :] + _rms(y, gpost_ref[...])


def _swiglu(x, params, tm=BIG_ROW_TILE):
    t, d = x.shape
    arrays, specs = _split(params)
    return pl.pallas_call(
        _swiglu_kernel,
        grid=(t // tm,),
        in_specs=[pl.BlockSpec((tm, d), lambda i: (i, 0))] + specs,
        out_specs=pl.BlockSpec((tm, d), lambda i: (i, 0)),
        out_shape=jax.ShapeDtypeStruct((t, d), F32),
        scratch_shapes=[pltpu.VMEM((tm, D_FF), BF16)],
        compiler_params=_params("parallel"),
        name="swiglu",
    )(x, *arrays)


def _pad_last(w, n):
    return jnp.pad(w, [(0, 0)] * (w.ndim - 1) + [(0, n - w.shape[-1])])


def _swap_halves(w):
    half = w.shape[-1] // 2
    return jnp.concatenate([w[..., half:], w[..., :half]], axis=-1)


def _transpose(w):
    return jnp.swapaxes(w, -1, -2)


def _mla_weights(w_in, w_uq, w_ukv):
    n = w_in.shape[0]
    r = MLA_Q_RANK + MLA_KV_RANK
    w_kr = w_in[..., r:]
    win = jnp.concatenate([w_in[..., :r], _pad_last(w_kr, LANE), _pad_last(_swap_halves(w_kr), LANE)], axis=-1)
    uq = w_uq.reshape(n, MLA_Q_RANK, MLA_HEADS, MLA_NOPE + MLA_ROPE)
    wqn = uq[..., :MLA_NOPE].reshape(n, MLA_Q_RANK, MLA_HEADS * LANE)
    rope = uq[..., MLA_NOPE:]
    wqr = rope.reshape(n, MLA_Q_RANK, MLA_HEADS * MLA_ROPE)
    wqs = _swap_halves(rope).reshape(n, MLA_Q_RANK, MLA_HEADS * MLA_ROPE)
    ukv = w_ukv.reshape(n, MLA_KV_RANK, MLA_HEADS, 2, MLA_NOPE)
    wk = ukv[:, :, :, 0, :].reshape(n, MLA_KV_RANK, MLA_HEADS * MLA_NOPE)
    wvt = _transpose(ukv[:, :, :, 1, :].reshape(n, MLA_KV_RANK, MLA_HEADS * MLA_V))
    return tuple(w.astype(BF16) for w in (win, _transpose(wqn), _transpose(wqr), _transpose(wqs), wk, wvt))


def _mlstm_weights(w_in, b_gates):
    nq = MLSTM_HEADS * MLSTM_DK
    nv = MLSTM_HEADS * MLSTM_DV
    w_q, w_k = w_in[..., :nq], w_in[..., nq:2 * nq]
    w_vo = w_in[..., 2 * nq:2 * nq + 2 * nv]
    w_g = w_in[..., 2 * nq + 2 * nv:]
    w_t = _transpose(jnp.concatenate([w_q, w_vo], axis=-1)).astype(BF16)
    return w_k.astype(BF16), w_t, _transpose(w_g).astype(BF16), b_gates[:, :, None]


def kernel(x, mem, positions, mla_w_in, mla_q_norm, mla_w_uq, mla_kv_norm, mla_w_ukv, mla_w_o, mlstm_w_in, mlstm_b_gates, mlstm_head_norm, mlstm_w_o, norm_mix_pre, norm_mix_post, norm_mem_q, norm_mem_kv, norm_mem_post, norm_ffn_pre, norm_ffn_post, mem_w_q, mem_w_kv, mem_w_o, ffn_w_gate_up, ffn_w_down):
    batch, seq, d = x.shape
    t = batch * seq
    xf = x.reshape(t, d)
    memf = mem.reshape(batch * mem.shape[1], d)
    tables = _rope_tables(positions.reshape(1, t))

    gain = lambda g: g[:, None, :]
    bf16 = lambda w: w.astype(BF16)
    g_mix_pre, g_mix_post = gain(norm_mix_pre), gain(norm_mix_post)
    g_mem_q, g_mem_kv, g_mem_post = gain(norm_mem_q), gain(norm_mem_kv), gain(norm_mem_post)
    g_ffn_pre, g_ffn_post = gain(norm_ffn_pre), gain(norm_ffn_post)
    mla_w = _mla_weights(mla_w_in, mla_w_uq, mla_w_ukv)
    mla_gq, mla_gkv, mla_wo = gain(mla_q_norm), gain(mla_kv_norm), bf16(mla_w_o)
    mlstm_w = _mlstm_weights(mlstm_w_in, mlstm_b_gates)
    mlstm_hn, mlstm_wo = mlstm_head_norm[..., None], bf16(mlstm_w_o)
    mem_wq, mem_wkv, mem_wo = bf16(mem_w_q), bf16(mem_w_kv), bf16(mem_w_o)
    ffn_wgu, ffn_wd = bf16(ffn_w_gate_up), bf16(ffn_w_down)
    mem_kv = _mem_kv(memf, g_mem_kv, mem_wkv)

    for i in range(DEPTH):
        j = i // 2
        if i % 2 == 0:
            win, wqn, wqr, wqs, wk, wvt = mla_w
            qt, k, vt = _mla_proj(xf, tables, [(g_mix_pre, i), (win, j), (mla_gq, j), (wqn, j), (wqr, j),
                                               (wqs, j), (mla_gkv, j), (wk, j), (wvt, j)])
            a = _mla_attn(qt, k, vt, batch, seq)
            w_mix = mla_wo
        else:
            wk, wt, wgt, bgt = mlstm_w
            k, qt, vt, sgt, gb, cc = _mlstm_proj(xf, [(g_mix_pre, i), (wk, j), (wt, j), (wgt, j), (bgt, j)])
            a = _mlstm_scan(k, qt, vt, sgt, gb, cc, (mlstm_hn, j), batch, seq)
            w_mix = mlstm_wo

        xf = _mix_out_mem_xattn(a, xf, mem_kv, i, [(w_mix, j), (g_mix_post, i), (g_mem_q, i), (mem_wq, i),
                                                   (mem_wo, i), (g_mem_post, i)], batch, seq)
        xf = _swiglu(xf, [(g_ffn_pre, i), (ffn_wgu, i), (ffn_wd, i), (g_ffn_post, i)])
    return xf.reshape(batch, seq, d)
```
